```python
import math
import jax, jax.numpy as jnp
from jax import lax
import numpy as np


D_MODEL = 1024
BATCH = 2
SEQ = 16384
DEPTH = 4

N_A_LAYERS = DEPTH // 2
N_B_LAYERS = DEPTH - N_A_LAYERS
SSM_GROUP = 16
SSM_GROUPS = D_MODEL // SSM_GROUP
SSM_STATE = 64
SSM_CHUNK = 128
DT_MIN = 1e-3
DT_MAX = 1e-1
HEAD_DIM = 64
N_HEADS = D_MODEL // HEAD_DIM
N_KV_HEADS = N_HEADS // 4
GQA_GROUP = N_HEADS // N_KV_HEADS
WINDOW = 128
ATTN_BLOCK = 128
ROPE_THETA = 500000.0
ROT_DIM = HEAD_DIM // 4
D_FF = 4 * D_MODEL
PLE_DIM = 256
RMS_EPS = 1e-6
NEG_INF = -1e30

kernel_name = 'yoco_s5_swa_sink_hybrid'


def rmsnorm(x, g):
    xf = x.astype(jnp.float32)
    y = xf * lax.rsqrt(jnp.mean(xf * xf, axis=-1, keepdims=True) + RMS_EPS)
    return (y * g.astype(jnp.float32)).astype(x.dtype)


def partial_rope(x, pos):
    inv = ROPE_THETA ** (-jnp.arange(0, ROT_DIM, 2, dtype=jnp.float32) / ROT_DIM)
    ang = pos.astype(jnp.float32)[:, None] * inv[None, :]
    cos = jnp.cos(ang)[None, :, None, :]
    sin = jnp.sin(ang)[None, :, None, :]
    xr = x[..., :ROT_DIM].astype(jnp.float32)
    x1, x2 = xr[..., :ROT_DIM // 2], xr[..., ROT_DIM // 2:]
    rot = jnp.concatenate([x1 * cos - x2 * sin, x2 * cos + x1 * sin], axis=-1).astype(x.dtype)
    return jnp.concatenate([rot, x[..., ROT_DIM:]], axis=-1)


def s5_mixer(u, lam_re, lam_im, log_dt, b_re, b_im, c_re, c_im, d, w_glu):
    bsz, seqlen, dm = u.shape
    f32 = jnp.float32
    lam_re = lam_re.astype(f32); lam_im = lam_im.astype(f32)
    b_re = b_re.astype(f32); b_im = b_im.astype(f32)
    c_re = c_re.astype(f32); c_im = c_im.astype(f32)
    dt = jnp.exp(log_dt.astype(f32))[:, None]
    mag = jnp.exp(lam_re * dt)
    a_r = mag * jnp.cos(lam_im * dt)
    a_i = mag * jnp.sin(lam_im * dt)
    den = lam_re * lam_re + lam_im * lam_im
    nr = a_r - 1.0
    coef_r = (nr * lam_re + a_i * lam_im) / den
    coef_i = (a_i * lam_re - nr * lam_im) / den
    bb_r = coef_r[..., None] * b_re - coef_i[..., None] * b_im
    bb_i = coef_r[..., None] * b_im + coef_i[..., None] * b_re

    n_chunks = seqlen // SSM_CHUNK
    ug = u.astype(f32).reshape(bsz, n_chunks, SSM_CHUNK, SSM_GROUPS, SSM_GROUP)
    ug = jnp.transpose(ug, (1, 0, 2, 3, 4))

    def combine(e1, e2):
        a1r, a1i, s1r, s1i = e1
        a2r, a2i, s2r, s2i = e2
        return (a2r * a1r - a2i * a1i,
                a2r * a1i + a2i * a1r,
                a2r * s1r - a2i * s1i + s2r,
                a2r * s1i + a2i * s1r + s2i)

    def chunk_step(carry, uc):
        cr, ci = carry
        bur = jnp.einsum('gnh,btgh->btgn', bb_r, uc)
        bui = jnp.einsum('gnh,btgh->btgn', bb_i, uc)
        ar = jnp.broadcast_to(a_r, bur.shape)
        ai = jnp.broadcast_to(a_i, bui.shape)
        pr, pi, sr, si = lax.associative_scan(combine, (ar, ai, bur, bui), axis=1)
        xr = sr + pr * cr[:, None] - pi * ci[:, None]
        xi = si + pr * ci[:, None] + pi * cr[:, None]
        y = jnp.einsum('ghn,btgn->btgh', c_re, xr) - jnp.einsum('ghn,btgn->btgh', c_im, xi)
        return (xr[:, -1], xi[:, -1]), y

    init = (jnp.zeros((bsz, SSM_GROUPS, SSM_STATE), f32), jnp.zeros((bsz, SSM_GROUPS, SSM_STATE), f32))
    _, ys = lax.scan(chunk_step, init, ug)
    y = jnp.transpose(ys, (1, 0, 2, 3, 4)).reshape(bsz, seqlen, dm)
    y = y + d.astype(f32) * u.astype(f32)
    y = jax.nn.gelu(y).astype(u.dtype)
    ab = y @ w_glu
    a, b = ab[..., :dm], ab[..., dm:]
    return a * jax.nn.sigmoid(b)


def swa_sink_attention(q, k, v, sinks):
    bsz, seqlen = q.shape[0], q.shape[1]
    nb = seqlen // ATTN_BLOCK
    f32 = jnp.float32
    qb = q.astype(f32).reshape(bsz, nb, ATTN_BLOCK, N_KV_HEADS, GQA_GROUP, HEAD_DIM)
    kb = k.astype(f32).reshape(bsz, nb, ATTN_BLOCK, N_KV_HEADS, HEAD_DIM)
    vb = v.astype(f32).reshape(bsz, nb, ATTN_BLOCK, N_KV_HEADS, HEAD_DIM)
    k_prev = jnp.concatenate([jnp.zeros_like(kb[:, :1]), kb[:, :-1]], axis=1)
    v_prev = jnp.concatenate([jnp.zeros_like(vb[:, :1]), vb[:, :-1]], axis=1)
    kk = jnp.concatenate([k_prev, kb], axis=2)
    vv = jnp.concatenate([v_prev, vb], axis=2)
    s = jnp.einsum('bnqkgd,bnjkd->bnkgqj', qb, kk) * (HEAD_DIM ** -0.5)
    qi = jnp.arange(ATTN_BLOCK)[:, None] + ATTN_BLOCK
    kj = jnp.arange(2 * ATTN_BLOCK)[None, :]
    band = (kj <= qi) & (qi - kj < WINDOW)
    has_prev = jnp.arange(nb) > 0
    mask = band[None] & ((kj >= ATTN_BLOCK)[None] | has_prev[:, None, None])
    s = jnp.where(mask[None, :, None, None], s, NEG_INF)
    sink = sinks.astype(f32).reshape(1, 1, N_KV_HEADS, GQA_GROUP, 1, 1)
    m = jnp.maximum(jnp.max(s, axis=-1, keepdims=True), sink)
    pr = jnp.exp(s - m)
    w = pr / (jnp.sum(pr, axis=-1, keepdims=True) + jnp.exp(sink - m))
    o = jnp.einsum('bnkgqj,bnjkd->bnqkgd', w, vv)
    return o.reshape(bsz, seqlen, N_HEADS * HEAD_DIM).astype(q.dtype)


def setup_inputs(seed: int = 0) -> dict:
    key = jax.random.key(seed)
    ks = jax.random.split(key, 32)
    f32 = jnp.float32
    nrm = lambda k, shape, scale: jax.random.normal(k, shape, f32) * scale
    x = nrm(ks[0], (BATCH, SEQ, D_MODEL), 1.0)
    p = nrm(ks[1], (DEPTH, BATCH, SEQ, PLE_DIM), 1.0)
    norm_mix = 1.0 + nrm(ks[2], (DEPTH, D_MODEL), 0.02)
    ssm_lambda_re = -0.5 + nrm(ks[3], (N_A_LAYERS, SSM_GROUPS, SSM_STATE), 0.01)
    ssm_lambda_im = (jnp.pi * jnp.arange(SSM_STATE, dtype=f32))[None, None, :] + nrm(ks[4], (N_A_LAYERS, SSM_GROUPS, SSM_STATE), 0.01)
    ssm_log_dt = math.log(DT_MIN) + jax.random.uniform(ks[5], (N_A_LAYERS, SSM_GROUPS), f32) * (math.log(DT_MAX) - math.log(DT_MIN))
    ssm_b_re = nrm(ks[6], (N_A_LAYERS, SSM_GROUPS, SSM_STATE, SSM_GROUP), (2 * SSM_GROUP) ** -0.5)
    ssm_b_im = nrm(ks[7], (N_A_LAYERS, SSM_GROUPS, SSM_STATE, SSM_GROUP), (2 * SSM_GROUP) ** -0.5)
    ssm_c_re = nrm(ks[8], (N_A_LAYERS, SSM_GROUPS, SSM_GROUP, SSM_STATE), (2 * SSM_STATE) ** -0.5)
    ssm_c_im = nrm(ks[9], (N_A_LAYERS, SSM_GROUPS, SSM_GROUP, SSM_STATE), (2 * SSM_STATE) ** -0.5)
    ssm_d = nrm(ks[10], (N_A_LAYERS, D_MODEL), 1.0)
    ssm_w_glu = nrm(ks[11], (N_A_LAYERS, D_MODEL, 2 * D_MODEL), D_MODEL ** -0.5)
    kv_norm = 1.0 + nrm(ks[12], (D_MODEL,), 0.02)
    w_k = nrm(ks[13], (D_MODEL, N_KV_HEADS * HEAD_DIM), D_MODEL ** -0.5)
    w_v = nrm(ks[14], (D_MODEL, N_KV_HEADS * HEAD_DIM), D_MODEL ** -0.5)
    w_q = nrm(ks[15], (N_B_LAYERS, D_MODEL, N_HEADS * HEAD_DIM), D_MODEL ** -0.5)
    attn_sinks = nrm(ks[16], (N_B_LAYERS, N_HEADS), 0.5)
    w_o = nrm(ks[17], (N_B_LAYERS, N_HEADS * HEAD_DIM, D_MODEL), (N_HEADS * HEAD_DIM) ** -0.5)
    norm_mlp = 1.0 + nrm(ks[18], (DEPTH, D_MODEL), 0.02)
    w_up = nrm(ks[19], (DEPTH, D_MODEL, D_FF), D_MODEL ** -0.5)
    w_down = nrm(ks[20], (DEPTH, D_FF, D_MODEL), D_FF ** -0.5)
    norm_ple = 1.0 + nrm(ks[21], (DEPTH, D_MODEL), 0.02)
    w_ple_gate = nrm(ks[22], (DEPTH, D_MODEL, D_MODEL), D_MODEL ** -0.5)
    w_ple_proj = nrm(ks[23], (DEPTH, PLE_DIM, D_MODEL), PLE_DIM ** -0.5)
    norm_final = 1.0 + nrm(ks[24], (D_MODEL,), 0.02)
    return {'x': x, 'p': p, 'norm_mix': norm_mix,
            'ssm_lambda_re': ssm_lambda_re, 'ssm_lambda_im': ssm_lambda_im, 'ssm_log_dt': ssm_log_dt,
            'ssm_b_re': ssm_b_re, 'ssm_b_im': ssm_b_im, 'ssm_c_re': ssm_c_re, 'ssm_c_im': ssm_c_im,
            'ssm_d': ssm_d, 'ssm_w_glu': ssm_w_glu,
            'kv_norm': kv_norm, 'w_k': w_k, 'w_v': w_v, 'w_q': w_q, 'attn_sinks': attn_sinks, 'w_o': w_o,
            'norm_mlp': norm_mlp, 'w_up': w_up, 'w_down': w_down,
            'norm_ple': norm_ple, 'w_ple_gate': w_ple_gate, 'w_ple_proj': w_ple_proj,
            'norm_final': norm_final}


def reference(x, p, norm_mix, ssm_lambda_re, ssm_lambda_im, ssm_log_dt, ssm_b_re, ssm_b_im,
              ssm_c_re, ssm_c_im, ssm_d, ssm_w_glu, kv_norm, w_k, w_v, w_q, attn_sinks, w_o,
              norm_mlp, w_up, w_down, norm_ple, w_ple_gate, w_ple_proj, norm_final):
    bsz, seqlen, _ = x.shape
    pos = jnp.arange(seqlen, dtype=jnp.int32)
    h = x
    k_shared = None
    v_shared = None
    for i in range(DEPTH):
        hn = rmsnorm(h, norm_mix[i])
        if i < N_A_LAYERS:
            mix = s5_mixer(hn, ssm_lambda_re[i], ssm_lambda_im[i], ssm_log_dt[i], ssm_b_re[i], ssm_b_im[i],
                           ssm_c_re[i], ssm_c_im[i], ssm_d[i], ssm_w_glu[i])
        else:
            j = i - N_A_LAYERS
            q = (hn @ w_q[j]).reshape(bsz, seqlen, N_HEADS, HEAD_DIM)
            q = partial_rope(q, pos)
            mix = swa_sink_attention(q, k_shared, v_shared, attn_sinks[j]) @ w_o[j]
        h = h + mix
        hm = rmsnorm(h, norm_mlp[i])
        h = h + jnp.square(jax.nn.relu(hm @ w_up[i])) @ w_down[i]
        gate = jax.nn.sigmoid(rmsnorm(h, norm_ple[i]) @ w_ple_gate[i])
        h = h + gate * (p[i] @ w_ple_proj[i])
        if i == N_A_LAYERS - 1:
            hk = rmsnorm(h, kv_norm)
            k_shared = partial_rope((hk @ w_k).reshape(bsz, seqlen, N_KV_HEADS, HEAD_DIM), pos)
            v_shared = (hk @ w_v).reshape(bsz, seqlen, N_KV_HEADS, HEAD_DIM)
    return rmsnorm(h, norm_final)
```

```python
import functools
import math

import jax
import jax.numpy as jnp
import numpy as np
from jax import lax
from jax.experimental import pallas as pl
from jax.experimental.pallas import tpu as pltpu

SSM_GROUP = 16
SSM_STATE = 64
HEAD_DIM = 64
N_KV_HEADS = 4
GQA_GROUP = 4
ATTN_BLOCK = 128
WINDOW = 128
ROPE_THETA = 500000.0
ROT_DIM = 16
RMS_EPS = 1e-6
NEG_INF = -1e30

LANES = 128
SUBLANES = 8
VMEM_LIMIT_BYTES = 56 * 1024 * 1024

S5_CHUNK = 16
TOKEN_BLOCK = 256
FF_CHUNK = 1024

BF16 = jnp.bfloat16
F32 = jnp.float32


def _dot(a, b):
    return jnp.dot(a, b, preferred_element_type=F32)


def _rms(x, g):
    return x * lax.rsqrt(jnp.mean(x * x, axis=-1, keepdims=True) + RMS_EPS) * g


def _gelu_tanh(x):
    c = math.sqrt(2.0 / math.pi)
    return 0.5 * x * (1.0 + jnp.tanh(c * (x + 0.044715 * (x * x * x))))


def _resident(shape):
    nd = len(shape)
    return pl.BlockSpec(shape, lambda i: (0,) * nd, pipeline_mode=pl.Buffered(1))


def _s5_tables(lam_re, lam_im, log_dt, b_re, b_im, c_re, c_im):
    hp = lax.Precision.HIGHEST
    L = S5_CHUNK
    G, N = lam_re.shape
    H = b_re.shape[-1]
    dt = jnp.exp(log_dt)[:, None]
    mag = jnp.exp(lam_re * dt)
    a_r = mag * jnp.cos(lam_im * dt)
    a_i = mag * jnp.sin(lam_im * dt)
    den = lam_re * lam_re + lam_im * lam_im
    nr = a_r - 1.0
    coef_r = (nr * lam_re + a_i * lam_im) / den
    coef_i = (a_i * lam_re - nr * lam_im) / den
    bb_r = coef_r[..., None] * b_re - coef_i[..., None] * b_im
    bb_i = coef_r[..., None] * b_im + coef_i[..., None] * b_re

    def cpow(base_r, base_i, n):
        rs, is_ = [jnp.ones_like(base_r)], [jnp.zeros_like(base_i)]
        for _ in range(n):
            r, i = rs[-1], is_[-1]
            rs.append(r * base_r - i * base_i)
            is_.append(r * base_i + i * base_r)
        return jnp.stack(rs), jnp.stack(is_)

    pw_r, pw_i = cpow(a_r, a_i, L)
    ab_r = pw_r[:L, :, :, None] * bb_r - pw_i[:L, :, :, None] * bb_i
    ab_i = pw_r[:L, :, :, None] * bb_i + pw_i[:L, :, :, None] * bb_r
    k = (jnp.einsum('gkn,dgnh->dgkh', c_re, ab_r, precision=hp)
         - jnp.einsum('gkn,dgnh->dgkh', c_im, ab_i, precision=hp))
    jj = np.arange(L)[:, None]
    ii = np.arange(L)[None, :]
    lag = np.clip(ii - jj, 0, L - 1)
    t = k[lag]
    t = jnp.where((ii >= jj)[:, :, None, None, None], t, 0.0)
    t = jnp.transpose(t, (2, 0, 4, 1, 3)).reshape(G, L * H, L * H)

    p_r = jnp.transpose(ab_r[::-1], (1, 0, 3, 2)).reshape(G // 2, 2, L * H, N)
    p_i = jnp.transpose(ab_i[::-1], (1, 0, 3, 2)).reshape(G // 2, 2, L * H, N)
    pb = jnp.zeros((G // 2, 2, L * H, 2, 2, N), F32)
    for a in range(2):
        pb = pb.at[:, a, :, 0, a, :].set(p_r[:, a])
        pb = pb.at[:, a, :, 1, a, :].set(p_i[:, a])
    pb = pb.reshape(G // 2, 2 * L * H, 4 * N)

    ca_r = c_re[None] * pw_r[1:, :, None, :] - c_im[None] * pw_i[1:, :, None, :]
    ca_i = c_re[None] * pw_i[1:, :, None, :] + c_im[None] * pw_r[1:, :, None, :]
    q_r = jnp.transpose(ca_r, (1, 3, 0, 2)).reshape(G // 2, 2, N, L * H)
    q_i = jnp.transpose(-ca_i, (1, 3, 0, 2)).reshape(G // 2, 2, N, L * H)
    qb = jnp.zeros((G // 2, 2, 2, N, 2, L * H), F32)
    for a in range(2):
        qb = qb.at[:, 0, a, :, a, :].set(q_r[:, a])
        qb = qb.at[:, 1, a, :, a, :].set(q_i[:, a])
    qb = qb.reshape(G // 2, 4 * N, 2 * L * H)

    al_r, al_i = cpow(pw_r[L], pw_i[L], SUBLANES)
    rows = [1, 2, 3, 4, 5, 6, 7, 8, 1, 2, 4, 8, 0, 0, 0, 0]
    sc = jnp.stack([al_r[jnp.array(rows)], al_i[jnp.array(rows)]])
    sc = jnp.transpose(sc.reshape(2, 16, G // 2, 2 * N), (2, 0, 1, 3))
    return t.astype(BF16), pb.astype(BF16), qb.astype(BF16), sc


def _s5_core_kernel(u_ref, t_ref, p_ref, q_ref, sc_ref, y_ref, v_scr, x_scr, *, tiles_per_seq):
    m = u_ref.shape[1]
    half = LANES
    row_chunk = min(m, 512)

    for r0 in range(0, m, row_chunk):
        v_scr[r0:r0 + row_chunk, :] = _dot(u_ref[0, r0:r0 + row_chunk, :], p_ref[0])

    row = lax.broadcasted_iota(jnp.int32, (SUBLANES, half), 0)
    pw_r = sc_ref[0, 0, 0:SUBLANES, :]
    pw_i = sc_ref[0, 1, 0:SUBLANES, :]
    step_r = [sc_ref[0, 0, 8 + s:9 + s, :] for s in range(4)]
    step_i = [sc_ref[0, 1, 8 + s:9 + s, :] for s in range(4)]

    def tile_body(t, carry):
        cr, ci = carry
        keep = (t % tiles_per_seq) != 0
        cr = jnp.where(keep, cr, 0.0)
        ci = jnp.where(keep, ci, 0.0)
        r0 = pl.multiple_of(t * SUBLANES, SUBLANES)
        r = v_scr[pl.ds(r0, SUBLANES), 0:half]
        i = v_scr[pl.ds(r0, SUBLANES), half:2 * half]
        for s_idx, s in enumerate((1, 2, 4)):
            ar, ai = step_r[s_idx], step_i[s_idx]
            rs = jnp.where(row >= s, pltpu.roll(r, s, axis=0), 0.0)
            is_ = jnp.where(row >= s, pltpu.roll(i, s, axis=0), 0.0)
            r, i = r + (ar * rs - ai * is_), i + (ar * is_ + ai * rs)
        inc_r = r + (pw_r * cr - pw_i * ci)
        inc_i = i + (pw_r * ci + pw_i * cr)
        x_scr[pl.ds(r0, SUBLANES), 0:half] = jnp.where(row == 0, cr, pltpu.roll(inc_r, 1, axis=0))
        x_scr[pl.ds(r0, SUBLANES), half:2 * half] = jnp.where(row == 0, ci, pltpu.roll(inc_i, 1, axis=0))
        last_r = jnp.broadcast_to(r[SUBLANES - 1:SUBLANES, :], (SUBLANES, half))
        last_i = jnp.broadcast_to(i[SUBLANES - 1:SUBLANES, :], (SUBLANES, half))
        a8r, a8i = step_r[3], step_i[3]
        return (last_r + (a8r * cr - a8i * ci), last_i + (a8r * ci + a8i * cr))

    zero = jnp.zeros((SUBLANES, half), F32)
    lax.fori_loop(0, m // SUBLANES, tile_body, (zero, zero), unroll=2)

    kk = t_ref.shape[1]
    for r0 in range(0, m, row_chunk):
        rows = slice(r0, r0 + row_chunk)
        carry_y = _dot(x_scr[rows, :].astype(BF16), q_ref[0])
        y_ref[0, rows, 0:kk] = _dot(u_ref[0, rows, 0:kk], t_ref[0]) + carry_y[:, 0:kk]
        y_ref[0, rows, kk:2 * kk] = _dot(u_ref[0, rows, kk:2 * kk], t_ref[1]) + carry_y[:, kk:2 * kk]


def _s5_core(u, t, pb, qb, sc, *, chunks_per_seq):
    gp, m, kw = u.shape
    kern = functools.partial(_s5_core_kernel, tiles_per_seq=chunks_per_seq // SUBLANES)
    return pl.pallas_call(
        kern,
        grid=(gp,),
        in_specs=[
            pl.BlockSpec((1, m, kw), lambda g: (g, 0, 0)),
            pl.BlockSpec((2, kw // 2, kw // 2), lambda g: (g, 0, 0)),
            pl.BlockSpec((1, kw, kw // 2), lambda g: (g, 0, 0)),
            pl.BlockSpec((1, kw // 2, kw), lambda g: (g, 0, 0)),
            pl.BlockSpec((1, 2, 16, LANES), lambda g: (g, 0, 0, 0)),
        ],
        out_specs=pl.BlockSpec((1, m, kw), lambda g: (g, 0, 0)),
        out_shape=jax.ShapeDtypeStruct((gp, m, kw), F32),
        scratch_shapes=[pltpu.VMEM((m, 2 * LANES), F32), pltpu.VMEM((m, 2 * LANES), F32)],
        compiler_params=pltpu.CompilerParams(
            dimension_semantics=("arbitrary",), vmem_limit_bytes=VMEM_LIMIT_BYTES),
        name="s5_core",
    )(u, t, pb, qb, sc)


def _prenorm_kernel(x_ref, g_ref, o_ref):
    o_ref[...] = _rms(x_ref[...], g_ref[...]).astype(o_ref.dtype)


def _prenorm(x, g):
    t, d = x.shape
    tb = min(t, 1024)
    return pl.pallas_call(
        _prenorm_kernel,
        grid=(t // tb,),
        in_specs=[pl.BlockSpec((tb, d), lambda i: (i, 0)), pl.BlockSpec((1, d), lambda i: (0, 0))],
        out_specs=pl.BlockSpec((tb, d), lambda i: (i, 0)),
        out_shape=jax.ShapeDtypeStruct((t, d), BF16),
        compiler_params=pltpu.CompilerParams(dimension_semantics=("arbitrary",)),
        name="prenorm",
    )(x, g)


def _rope(x, cos_t, sin_t, l64):
    half = ROT_DIM // 2
    lo = pltpu.roll(x, half, axis=1)
    hi = pltpu.roll(x, LANES - half, axis=1)
    return x * cos_t + jnp.where(l64 < half, -hi, lo) * sin_t


def _rope_tables(inv_ref, tb, seq):
    pos = (pl.program_id(0) * tb + lax.broadcasted_iota(jnp.int32, (tb, LANES), 0)) % seq
    ang = pos.astype(F32) * inv_ref[...]
    l64 = lax.broadcasted_iota(jnp.int32, (tb, LANES), 1) % HEAD_DIM
    return jnp.cos(ang), jnp.sin(ang), l64


def _attention(q_ref, kc_ref, kp_ref, vc_ref, vp_ref, sink_ref, o_scr, *, tb, seq):
    nb = tb // ATTN_BLOCK
    blocks_per_seq = seq // ATTN_BLOCK
    width = 2 * ATTN_BLOCK
    qi = lax.broadcasted_iota(jnp.int32, (ATTN_BLOCK, 2 * width), 0) + ATTN_BLOCK
    kj = lax.broadcasted_iota(jnp.int32, (ATTN_BLOCK, 2 * width), 1) % width
    band = (kj <= qi) & (qi - kj < WINDOW)
    lane_k = lax.broadcasted_iota(jnp.int32, (width, LANES), 1)
    lane_o = lax.broadcasted_iota(jnp.int32, (ATTN_BLOCK, LANES), 1)
    for b in range(nb):
        rows = slice(b * ATTN_BLOCK, (b + 1) * ATTN_BLOCK)
        has_prev = ((pl.program_id(0) * nb + b) % blocks_per_seq) != 0
        first_key = jnp.where(has_prev, 0, ATTN_BLOCK)
        mask = band & (kj >= first_key)
        for kh in range(N_KV_HEADS):
            if b == 0:
                k_prev, v_prev = kp_ref[kh], vp_ref[kh]
            else:
                prev = slice((b - 1) * ATTN_BLOCK, b * ATTN_BLOCK)
                k_prev, v_prev = kc_ref[kh, prev, :], vc_ref[kh, prev, :]
            kcat = jnp.concatenate([k_prev, kc_ref[kh, rows, :]], axis=0)
            vcat = jnp.concatenate([v_prev, vc_ref[kh, rows, :]], axis=0)
            zk = jnp.zeros_like(kcat)
            k_blk = jnp.concatenate([jnp.where(lane_k < HEAD_DIM, kcat, zk),
                                     jnp.where(lane_k >= HEAD_DIM, kcat, zk)], axis=0)
            v_blk = jnp.concatenate([jnp.where(lane_k < HEAD_DIM, vcat, zk),
                                     jnp.where(lane_k >= HEAD_DIM, vcat, zk)], axis=0)
            for pair in range(GQA_GROUP // 2):
                hp = kh * (GQA_GROUP // 2) + pair
                s = lax.dot_general(q_ref[hp, rows, :], k_blk, (((1,), (1,)), ((), ())),
                                    preferred_element_type=F32)
                s = jnp.where(mask, s, NEG_INF)
                es, invs = [], []
                for hh in range(2):
                    sh = s[:, hh * width:(hh + 1) * width]
                    sink = sink_ref[2 * hp + hh]
                    mx = jnp.maximum(jnp.max(sh, axis=-1, keepdims=True), sink)
                    e = jnp.exp(sh - mx)
                    den = jnp.sum(e, axis=-1, keepdims=True) + jnp.exp(sink - mx)
                    es.append(e.astype(BF16))
                    invs.append(1.0 / den)
                o = _dot(jnp.concatenate(es, axis=1), v_blk)
                o = o * jnp.where(lane_o < HEAD_DIM, invs[0], invs[1])
                o_scr[rows, hp * LANES:(hp + 1) * LANES] = o.astype(o_scr.dtype)


def _layer_kernel(*refs, mixer, epilogue, tb, seq):
    it = iter(refs)
    h_ref = next(it)
    p_ref = next(it)
    if mixer == "s5":
        y_ref, gmix_ref, d_ref, wglu_ref = next(it), next(it), next(it), next(it)
    else:
        q_ref, kc_ref, kp_ref, vc_ref, vp_ref, sink_ref, wo_ref = (next(it) for _ in range(7))
    gmlp_ref, wup_ref, wdown_ref, gple_ref, wgate_ref, wproj_ref = (next(it) for _ in range(6))
    if epilogue == "norm":
        gnext_ref = next(it)
    elif epilogue == "q":
        gnext_ref, wq_ref, inv_ref = next(it), next(it), next(it)
    elif epilogue == "kvq":
        gnext_ref, wq_ref, inv_ref, gkv_ref, wk_ref, wv_ref = (next(it) for _ in range(6))
    elif epilogue == "final":
        gnext_ref = next(it)
    ho_ref = None if epilogue == "final" else next(it)
    if epilogue == "norm":
        hn_ref = next(it)
    elif epilogue == "q":
        qo_ref = next(it)
    elif epilogue == "kvq":
        qo_ref, ko_ref, vo_ref = next(it), next(it), next(it)
    elif epilogue == "final":
        out_ref = next(it)
    if mixer == "attn":
        o_scr = next(it)

    h = h_ref[...]
    d_model = h.shape[-1]

    if mixer == "s5":
        u = _rms(h, gmix_ref[...])
        ge = _gelu_tanh(y_ref[...] + d_ref[...] * u).astype(BF16)
        ab = _dot(ge, wglu_ref[...])
        h = h + ab[:, :d_model] * jax.nn.sigmoid(ab[:, d_model:])
    else:
        _attention(q_ref, kc_ref, kp_ref, vc_ref, vp_ref, sink_ref, o_scr, tb=tb, seq=seq)
        h = h + _dot(o_scr[...], wo_ref[...])

    hm = _rms(h, gmlp_ref[...]).astype(BF16)
    acc = h
    d_ff = wup_ref.shape[1]
    for c0 in range(0, d_ff, FF_CHUNK):
        up = _dot(hm, wup_ref[:, c0:c0 + FF_CHUNK])
        act = jnp.square(jnp.maximum(up, 0.0)).astype(BF16)
        acc = acc + _dot(act, wdown_ref[c0:c0 + FF_CHUNK, :])
    h = acc

    gate = jax.nn.sigmoid(_dot(_rms(h, gple_ref[...]).astype(BF16), wgate_ref[...]))
    h = h + gate * _dot(p_ref[...].astype(BF16), wproj_ref[...])

    if epilogue == "final":
        out_ref[...] = _rms(h, gnext_ref[...])
        return
    ho_ref[...] = h
    if epilogue == "norm":
        hn_ref[...] = _rms(h, gnext_ref[...]).astype(hn_ref.dtype)
        return
    cos_t, sin_t, l64 = _rope_tables(inv_ref, tb, seq)
    hq = _rms(h, gnext_ref[...]).astype(BF16)
    q = _dot(hq, wq_ref[...])
    scale = HEAD_DIM ** -0.5
    for hp in range(d_model // LANES):
        tile = _rope(q[:, hp * LANES:(hp + 1) * LANES], cos_t, sin_t, l64)
        qo_ref[hp] = (tile * scale).astype(qo_ref.dtype)
    if epilogue == "kvq":
        hk = _rms(h, gkv_ref[...]).astype(BF16)
        k = _dot(hk, wk_ref[...])
        v = _dot(hk, wv_ref[...])
        for kh in range(N_KV_HEADS):
            ko_ref[kh] = _rope(k[:, kh * LANES:(kh + 1) * LANES], cos_t, sin_t, l64).astype(ko_ref.dtype)
            vo_ref[kh] = v[:, kh * LANES:(kh + 1) * LANES].astype(vo_ref.dtype)


def _layer_call(h, p, mixer_args, tail_args, epi_args, *, mixer, epilogue, seq):
    t, d = h.shape
    tb = min(TOKEN_BLOCK, seq)
    nsteps = t // tb
    row = lambda w: pl.BlockSpec((tb, w), lambda i: (i, 0))
    vec = lambda: pl.BlockSpec((1, d), lambda i: (0, 0))
    heads = lambda n: pl.BlockSpec((n, tb, LANES), lambda i: (0, i, 0))

    args = [h, p]
    specs = [row(d), row(p.shape[1])]
    if mixer == "s5":
        y, gmix, dskip, wglu = mixer_args
        args += [y, gmix, dskip, wglu]
        specs += [row(d), vec(), vec(), _resident(wglu.shape)]
    else:
        q, kd, vd, sinks, wo = mixer_args
        blocks_per_tb = tb // ATTN_BLOCK
        prev = lambda: pl.BlockSpec((N_KV_HEADS, ATTN_BLOCK, LANES),
                                    lambda i: (0, jnp.maximum(i * blocks_per_tb - 1, 0), 0))
        args += [q, kd, kd, vd, vd, sinks, wo]
        specs += [heads(d // LANES), heads(N_KV_HEADS), prev(), heads(N_KV_HEADS), prev(),
                  pl.BlockSpec(memory_space=pltpu.SMEM), _resident(wo.shape)]
    gmlp, wup, wdown, gple, wgate, wproj = tail_args
    args += [gmlp, wup, wdown, gple, wgate, wproj]
    specs += [vec(), _resident(wup.shape), _resident(wdown.shape), vec(),
              _resident(wgate.shape), _resident(wproj.shape)]

    inv_spec = pl.BlockSpec((1, LANES), lambda i: (0, 0))
    out_shapes, out_specs = [], []
    if epilogue != "final":
        out_shapes.append(jax.ShapeDtypeStruct((t, d), F32))
        out_specs.append(row(d))
    if epilogue == "norm":
        (gnext,) = epi_args
        args += [gnext]
        specs += [vec()]
        out_shapes.append(jax.ShapeDtypeStruct((t, d), BF16))
        out_specs.append(row(d))
    elif epilogue == "q":
        gnext, wq, inv = epi_args
        args += [gnext, wq, inv]
        specs += [vec(), _resident(wq.shape), inv_spec]
        out_shapes.append(jax.ShapeDtypeStruct((d // LANES, t, LANES), BF16))
        out_specs.append(heads(d // LANES))
    elif epilogue == "kvq":
        gnext, wq, inv, gkv, wk, wv = epi_args
        args += [gnext, wq, inv, gkv, wk, wv]
        specs += [vec(), _resident(wq.shape), inv_spec, vec(), _resident(wk.shape), _resident(wv.shape)]
        out_shapes += [jax.ShapeDtypeStruct((d // LANES, t, LANES), BF16),
                       jax.ShapeDtypeStruct((N_KV_HEADS, t, LANES), BF16),
                       jax.ShapeDtypeStruct((N_KV_HEADS, t, LANES), BF16)]
        out_specs += [heads(d // LANES), heads(N_KV_HEADS), heads(N_KV_HEADS)]
    elif epilogue == "final":
        (gnext,) = epi_args
        args += [gnext]
        specs += [vec()]
        out_shapes.append(jax.ShapeDtypeStruct((t, d), F32))
        out_specs.append(row(d))

    scratch = [pltpu.VMEM((tb, d), BF16)] if mixer == "attn" else []
    kern = functools.partial(_layer_kernel, mixer=mixer, epilogue=epilogue, tb=tb, seq=seq)
    return pl.pallas_call(
        kern,
        grid=(nsteps,),
        in_specs=specs,
        out_specs=out_specs,
        out_shape=out_shapes,
        scratch_shapes=scratch,
        compiler_params=pltpu.CompilerParams(
            dimension_semantics=("arbitrary",), vmem_limit_bytes=VMEM_LIMIT_BYTES),
        name=f"layer_{mixer}_{epilogue}",
    )(*args)


def kernel(x, p, norm_mix, ssm_lambda_re, ssm_lambda_im, ssm_log_dt, ssm_b_re, ssm_b_im, ssm_c_re, ssm_c_im, ssm_d, ssm_w_glu, kv_norm, w_k, w_v, w_q, attn_sinks, w_o, norm_mlp, w_up, w_down, norm_ple, w_ple_gate, w_ple_proj, norm_final):
    bsz, seq, d = x.shape
    depth = p.shape[0]
    n_a = ssm_lambda_re.shape[0]
    groups = ssm_lambda_re.shape[1]
    t = bsz * seq
    L = S5_CHUNK
    m = t // L
    assert seq % (L * SUBLANES) == 0 and seq % ATTN_BLOCK == 0 and d % LANES == 0

    vec = lambda g: g.reshape(1, d)
    inv = ROPE_THETA ** (-jnp.arange(0, ROT_DIM, 2, dtype=F32) / ROT_DIM)
    inv64 = jnp.concatenate([inv, inv, jnp.zeros((HEAD_DIM - ROT_DIM,), F32)])
    inv_lanes = jnp.tile(inv64, LANES // HEAD_DIM).reshape(1, LANES)
    dup = lambda w: jnp.repeat(w.reshape(d, N_KV_HEADS, 1, HEAD_DIM), 2, axis=2).reshape(d, 2 * N_KV_HEADS * HEAD_DIM)

    h = x.reshape(t, d)
    hn = _prenorm(h, vec(norm_mix[0]))
    q = kd = vd = None
    out = None
    for i in range(depth):
        p_i = p[i].reshape(t, p.shape[-1])
        tail = (vec(norm_mlp[i]), w_up[i].astype(BF16), w_down[i].astype(BF16),
                vec(norm_ple[i]), w_ple_gate[i].astype(BF16), w_ple_proj[i].astype(BF16))
        last = i == depth - 1
        if last:
            epilogue, epi = "final", (vec(norm_final),)
        elif i + 1 < n_a:
            epilogue, epi = "norm", (vec(norm_mix[i + 1]),)
        elif i + 1 == n_a:
            epilogue = "kvq"
            epi = (vec(norm_mix[i + 1]), w_q[0].astype(BF16), inv_lanes,
                   vec(kv_norm), dup(w_k).astype(BF16), dup(w_v).astype(BF16))
        else:
            epilogue, epi = "q", (vec(norm_mix[i + 1]), w_q[i + 1 - n_a].astype(BF16), inv_lanes)

        if i < n_a:
            tt, pb, qb, sc = _s5_tables(ssm_lambda_re[i], ssm_lambda_im[i], ssm_log_dt[i],
                                        ssm_b_re[i], ssm_b_im[i], ssm_c_re[i], ssm_c_im[i])
            u = hn.reshape(m, L, groups // 2, 2, SSM_GROUP)
            u = jnp.transpose(u, (2, 0, 3, 1, 4)).reshape(groups // 2, m, 2 * L * SSM_GROUP)
            y = _s5_core(u, tt, pb, qb, sc, chunks_per_seq=seq // L)
            y = y.reshape(groups // 2, m, 2, L, SSM_GROUP)
            y = jnp.transpose(y, (1, 3, 0, 2, 4)).reshape(t, d)
            mixer, mix = "s5", (y, vec(norm_mix[i]), vec(ssm_d[i]), ssm_w_glu[i].astype(BF16))
        else:
            j = i - n_a
            mixer, mix = "attn", (q, kd, vd, attn_sinks[j], w_o[j].astype(BF16))

        res = _layer_call(h, p_i, mix, tail, epi, mixer=mixer, epilogue=epilogue, seq=seq)
        if epilogue == "final":
            out = res[0]
        elif epilogue == "norm":
            h, hn = res
        elif epilogue == "q":
            h, q = res
        else:
            h, q, kd, vd = res
    return out.reshape(bsz, seq, d)
```

```python
import functools
import math

import jax
import jax.numpy as jnp
import numpy as np
from jax import lax
from jax.experimental import pallas as pl
from jax.experimental.pallas import tpu as pltpu

SSM_GROUP = 16
SSM_STATE = 64
HEAD_DIM = 64
N_KV_HEADS = 4
GQA_GROUP = 4
ATTN_BLOCK = 128
WINDOW = 128
ROPE_THETA = 500000.0
ROT_DIM = 16
RMS_EPS = 1e-6
NEG_INF = -1e30

LANES = 128
SUBLANES = 8
VMEM_LIMIT_BYTES = 56 * 1024 * 1024

S5_CHUNK = 16
S5_ROWS = 256
TOKEN_BLOCK = 256
FF_CHUNK = 1024

BF16 = jnp.bfloat16
F32 = jnp.float32


def _dot(a, b):
    return jnp.dot(a, b, preferred_element_type=F32)


def _rms(x, g):
    return x * lax.rsqrt(jnp.mean(x * x, axis=-1, keepdims=True) + RMS_EPS) * g


def _gelu_tanh(x):
    c = math.sqrt(2.0 / math.pi)
    return 0.5 * x * (1.0 + jnp.tanh(c * (x + 0.044715 * (x * x * x))))


def _resident(shape):
    nd = len(shape)
    return pl.BlockSpec(shape, lambda i: (0,) * nd, pipeline_mode=pl.Buffered(1))


def _s5_tables(lam_re, lam_im, log_dt, b_re, b_im, c_re, c_im):
    hp = lax.Precision.HIGHEST
    L = S5_CHUNK
    G, N = lam_re.shape
    H = b_re.shape[-1]
    dt = jnp.exp(log_dt)[:, None]
    mag = jnp.exp(lam_re * dt)
    a_r = mag * jnp.cos(lam_im * dt)
    a_i = mag * jnp.sin(lam_im * dt)
    den = lam_re * lam_re + lam_im * lam_im
    nr = a_r - 1.0
    coef_r = (nr * lam_re + a_i * lam_im) / den
    coef_i = (a_i * lam_re - nr * lam_im) / den
    bb_r = coef_r[..., None] * b_re - coef_i[..., None] * b_im
    bb_i = coef_r[..., None] * b_im + coef_i[..., None] * b_re

    def cpow(base_r, base_i, n):
        rs, is_ = [jnp.ones_like(base_r)], [jnp.zeros_like(base_i)]
        for _ in range(n):
            r, i = rs[-1], is_[-1]
            rs.append(r * base_r - i * base_i)
            is_.append(r * base_i + i * base_r)
        return jnp.stack(rs), jnp.stack(is_)

    pw_r, pw_i = cpow(a_r, a_i, L)
    ab_r = pw_r[:L, :, :, None] * bb_r - pw_i[:L, :, :, None] * bb_i
    ab_i = pw_r[:L, :, :, None] * bb_i + pw_i[:L, :, :, None] * bb_r
    k = (jnp.einsum('gkn,dgnh->dgkh', c_re, ab_r, precision=hp)
         - jnp.einsum('gkn,dgnh->dgkh', c_im, ab_i, precision=hp))
    jj = np.arange(L)[:, None]
    ii = np.arange(L)[None, :]
    lag = np.clip(ii - jj, 0, L - 1)
    t = k[lag]
    t = jnp.where((ii >= jj)[:, :, None, None, None], t, 0.0)
    t = jnp.transpose(t, (2, 0, 4, 1, 3)).reshape(G, L * H, L * H)

    p_r = jnp.transpose(ab_r[::-1], (1, 0, 3, 2)).reshape(G // 2, 2, L * H, N)
    p_i = jnp.transpose(ab_i[::-1], (1, 0, 3, 2)).reshape(G // 2, 2, L * H, N)
    pb = jnp.zeros((G // 2, 2, L * H, 2, 2, N), F32)
    for a in range(2):
        pb = pb.at[:, a, :, 0, a, :].set(p_r[:, a])
        pb = pb.at[:, a, :, 1, a, :].set(p_i[:, a])
    pb = pb.reshape(G // 2, 2 * L * H, 4 * N)

    ca_r = c_re[None] * pw_r[1:, :, None, :] - c_im[None] * pw_i[1:, :, None, :]
    ca_i = c_re[None] * pw_i[1:, :, None, :] + c_im[None] * pw_r[1:, :, None, :]
    q_r = jnp.transpose(ca_r, (1, 3, 0, 2)).reshape(G // 2, 2, N, L * H)
    q_i = jnp.transpose(-ca_i, (1, 3, 0, 2)).reshape(G // 2, 2, N, L * H)
    qb = jnp.zeros((G // 2, 2, 2, N, 2, L * H), F32)
    for a in range(2):
        qb = qb.at[:, 0, a, :, a, :].set(q_r[:, a])
        qb = qb.at[:, 1, a, :, a, :].set(q_i[:, a])
    qb = qb.reshape(G // 2, 4 * N, 2 * L * H)

    al_r, al_i = cpow(pw_r[L], pw_i[L], SUBLANES)
    rows = [1, 2, 3, 4, 5, 6, 7, 8, 1, 2, 4, 8, 0, 0, 0, 0]
    sc = jnp.stack([al_r[jnp.array(rows)], al_i[jnp.array(rows)]])
    sc = jnp.transpose(sc.reshape(2, 16, G // 2, 2 * N), (2, 0, 1, 3))
    return t.astype(BF16), pb.astype(BF16), qb.astype(BF16), sc


def _block_transpose(vs, lane_blk):
    vs = list(vs)
    for d in (4, 2, 1):
        take = (lane_blk & d) != 0
        for r in range(len(vs)):
            if r & d:
                continue
            top, bot = vs[r], vs[r + d]
            vs[r] = jnp.where(take, pltpu.roll(bot, d * SSM_GROUP, axis=1), top)
            vs[r + d] = jnp.where(take, bot, pltpu.roll(top, LANES - d * SSM_GROUP, axis=1))
    return vs


def _s5_core_kernel(hn_ref, t_ref, p_ref, q_ref, sc_ref, y_ref, u_scr, v_scr, x_scr, yg_scr, carry_scr,
                    *, rows, tiles_per_seq):
    L = S5_CHUNK
    n_pair = p_ref.shape[0]
    kk = t_ref.shape[1]
    rb = pl.program_id(1)
    lane_blk = lax.broadcasted_iota(jnp.int32, (rows, LANES), 1) // SSM_GROUP

    for jh in range(L // SUBLANES):
        vs = [hn_ref[pl.ds(jh * SUBLANES + r, rows, stride=L), :] for r in range(SUBLANES)]
        vs = _block_transpose(vs, lane_blk)
        for gl in range(2 * n_pair):
            col = ((gl % 2) * (L // SUBLANES) + jh) * LANES
            u_scr[gl // 2, :, col:col + LANES] = vs[gl].astype(BF16)

    for gp in range(n_pair):
        v_scr[gp] = _dot(u_scr[gp], p_ref[gp])

    @pl.when(rb == 0)
    def _():
        carry_scr[...] = jnp.zeros_like(carry_scr)

    half = LANES
    row = lax.broadcasted_iota(jnp.int32, (SUBLANES, half), 0)
    tiles = rows // SUBLANES

    def tile_body(t, carry):
        keep = ((rb * tiles + t) % tiles_per_seq) != 0
        r0 = pl.multiple_of(t * SUBLANES, SUBLANES)
        out = []
        for gp in range(n_pair):
            cr = jnp.where(keep, carry[2 * gp], 0.0)
            ci = jnp.where(keep, carry[2 * gp + 1], 0.0)
            r = v_scr[gp, pl.ds(r0, SUBLANES), 0:half]
            i = v_scr[gp, pl.ds(r0, SUBLANES), half:2 * half]
            for s_idx, s in enumerate((1, 2, 4)):
                ar = sc_ref[gp, 0, 8 + s_idx:9 + s_idx, :]
                ai = sc_ref[gp, 1, 8 + s_idx:9 + s_idx, :]
                rs = jnp.where(row >= s, pltpu.roll(r, s, axis=0), 0.0)
                is_ = jnp.where(row >= s, pltpu.roll(i, s, axis=0), 0.0)
                r, i = r + (ar * rs - ai * is_), i + (ar * is_ + ai * rs)
            pw_r = sc_ref[gp, 0, 0:SUBLANES, :]
            pw_i = sc_ref[gp, 1, 0:SUBLANES, :]
            inc_r = r + (pw_r * cr - pw_i * ci)
            inc_i = i + (pw_r * ci + pw_i * cr)
            x_scr[gp, pl.ds(r0, SUBLANES), 0:half] = jnp.where(row == 0, cr, pltpu.roll(inc_r, 1, axis=0))
            x_scr[gp, pl.ds(r0, SUBLANES), half:2 * half] = jnp.where(row == 0, ci, pltpu.roll(inc_i, 1, axis=0))
            last_r = jnp.broadcast_to(r[SUBLANES - 1:SUBLANES, :], (SUBLANES, half))
            last_i = jnp.broadcast_to(i[SUBLANES - 1:SUBLANES, :], (SUBLANES, half))
            a8r = sc_ref[gp, 0, 11:12, :]
            a8i = sc_ref[gp, 1, 11:12, :]
            out += [last_r + (a8r * cr - a8i * ci), last_i + (a8r * ci + a8i * cr)]
        return tuple(out)

    init = tuple(carry_scr[k] for k in range(2 * n_pair))
    fin = lax.fori_loop(0, tiles, tile_body, init, unroll=2)
    for k in range(2 * n_pair):
        carry_scr[k] = fin[k]

    for gp in range(n_pair):
        carry_y = _dot(x_scr[gp].astype(BF16), q_ref[gp])
        for g2 in range(2):
            cols = slice(g2 * kk, (g2 + 1) * kk)
            yg_scr[2 * gp + g2] = _dot(u_scr[gp, :, cols], t_ref[2 * gp + g2]) + carry_y[:, cols]

    for ih in range(L // SUBLANES):
        vs = [yg_scr[gl, :, ih * LANES:(ih + 1) * LANES] for gl in range(2 * n_pair)]
        vs = _block_transpose(vs, lane_blk)
        for r in range(SUBLANES):
            y_ref[pl.ds(ih * SUBLANES + r, rows, stride=L), :] = vs[r]


def _s5_core(hn, t, pb, qb, sc, *, chunks_per_seq):
    tokens, d = hn.shape
    L = S5_CHUNK
    m = tokens // L
    rows = min(S5_ROWS, m)
    kk = L * SSM_GROUP
    n_pair = LANES // (2 * SSM_GROUP)
    kern = functools.partial(_s5_core_kernel, rows=rows, tiles_per_seq=chunks_per_seq // SUBLANES)
    per_tile = lambda *shape: pl.BlockSpec(shape, lambda lt, rb: (lt,) + (0,) * (len(shape) - 1))
    return pl.pallas_call(
        kern,
        grid=(d // LANES, m // rows),
        in_specs=[
            pl.BlockSpec((rows * L, LANES), lambda lt, rb: (rb, lt)),
            per_tile(2 * n_pair, kk, kk),
            per_tile(n_pair, 2 * kk, kk),
            per_tile(n_pair, kk, 2 * kk),
            per_tile(n_pair, 2, 16, LANES),
        ],
        out_specs=pl.BlockSpec((rows * L, LANES), lambda lt, rb: (rb, lt)),
        out_shape=jax.ShapeDtypeStruct((tokens, d), F32),
        scratch_shapes=[
            pltpu.VMEM((n_pair, rows, 2 * kk), BF16),
            pltpu.VMEM((n_pair, rows, 2 * LANES), F32),
            pltpu.VMEM((n_pair, rows, 2 * LANES), F32),
            pltpu.VMEM((2 * n_pair, rows, kk), F32),
            pltpu.VMEM((2 * n_pair, SUBLANES, LANES), F32),
        ],
        compiler_params=pltpu.CompilerParams(
            dimension_semantics=("arbitrary", "arbitrary"), vmem_limit_bytes=VMEM_LIMIT_BYTES),
        name="s5_core",
    )(hn, t, pb, qb, sc)


def _prenorm_kernel(x_ref, g_ref, o_ref):
    o_ref[...] = _rms(x_ref[...], g_ref[...]).astype(o_ref.dtype)


def _prenorm(x, g):
    t, d = x.shape
    tb = min(t, 1024)
    return pl.pallas_call(
        _prenorm_kernel,
        grid=(t // tb,),
        in_specs=[pl.BlockSpec((tb, d), lambda i: (i, 0)), pl.BlockSpec((1, d), lambda i: (0, 0))],
        out_specs=pl.BlockSpec((tb, d), lambda i: (i, 0)),
        out_shape=jax.ShapeDtypeStruct((t, d), F32),
        compiler_params=pltpu.CompilerParams(dimension_semantics=("arbitrary",)),
        name="prenorm",
    )(x, g)


def _rope(x, cos_t, sin_t, l64):
    half = ROT_DIM // 2
    lo = pltpu.roll(x, half, axis=1)
    hi = pltpu.roll(x, LANES - half, axis=1)
    return x * cos_t + jnp.where(l64 < half, -hi, lo) * sin_t


def _rope_tables(inv_ref, tb, seq):
    pos = (pl.program_id(0) * tb + lax.broadcasted_iota(jnp.int32, (tb, LANES), 0)) % seq
    ang = pos.astype(F32) * inv_ref[...]
    l64 = lax.broadcasted_iota(jnp.int32, (tb, LANES), 1) % HEAD_DIM
    return jnp.cos(ang), jnp.sin(ang), l64


def _attention(q_ref, kc_ref, kp_ref, vc_ref, vp_ref, sink_ref, o_scr, *, tb, seq):
    nb = tb // ATTN_BLOCK
    blocks_per_seq = seq // ATTN_BLOCK
    width = 2 * ATTN_BLOCK
    qi = lax.broadcasted_iota(jnp.int32, (ATTN_BLOCK, 2 * width), 0) + ATTN_BLOCK
    kj = lax.broadcasted_iota(jnp.int32, (ATTN_BLOCK, 2 * width), 1) % width
    band = (kj <= qi) & (qi - kj < WINDOW)
    lane_k = lax.broadcasted_iota(jnp.int32, (width, LANES), 1)
    lane_o = lax.broadcasted_iota(jnp.int32, (ATTN_BLOCK, LANES), 1)
    for b in range(nb):
        rows = slice(b * ATTN_BLOCK, (b + 1) * ATTN_BLOCK)
        has_prev = ((pl.program_id(0) * nb + b) % blocks_per_seq) != 0
        first_key = jnp.where(has_prev, 0, ATTN_BLOCK)
        mask = band & (kj >= first_key)
        for kh in range(N_KV_HEADS):
            if b == 0:
                k_prev, v_prev = kp_ref[kh], vp_ref[kh]
            else:
                prev = slice((b - 1) * ATTN_BLOCK, b * ATTN_BLOCK)
                k_prev, v_prev = kc_ref[kh, prev, :], vc_ref[kh, prev, :]
            kcat = jnp.concatenate([k_prev, kc_ref[kh, rows, :]], axis=0)
            vcat = jnp.concatenate([v_prev, vc_ref[kh, rows, :]], axis=0)
            zk = jnp.zeros_like(kcat)
            k_blk = jnp.concatenate([jnp.where(lane_k < HEAD_DIM, kcat, zk),
                                     jnp.where(lane_k >= HEAD_DIM, kcat, zk)], axis=0)
            v_blk = jnp.concatenate([jnp.where(lane_k < HEAD_DIM, vcat, zk),
                                     jnp.where(lane_k >= HEAD_DIM, vcat, zk)], axis=0)
            for pair in range(GQA_GROUP // 2):
                hp = kh * (GQA_GROUP // 2) + pair
                s = lax.dot_general(q_ref[hp, rows, :], k_blk, (((1,), (1,)), ((), ())),
                                    preferred_element_type=F32)
                s = jnp.where(mask, s, NEG_INF)
                es, invs = [], []
                for hh in range(2):
                    sh = s[:, hh * width:(hh + 1) * width]
                    sink = sink_ref[2 * hp + hh]
                    mx = jnp.maximum(jnp.max(sh, axis=-1, keepdims=True), sink)
                    e = jnp.exp(sh - mx)
                    den = jnp.sum(e, axis=-1, keepdims=True) + jnp.exp(sink - mx)
                    es.append(e.astype(BF16))
                    invs.append(1.0 / den)
                o = _dot(jnp.concatenate(es, axis=1), v_blk)
                o = o * jnp.where(lane_o < HEAD_DIM, invs[0], invs[1])
                o_scr[rows, hp * LANES:(hp + 1) * LANES] = o.astype(o_scr.dtype)


def _layer_kernel(*refs, mixer, epilogue, tb, seq):
    it = iter(refs)
    h_ref = next(it)
    p_ref = next(it)
    if mixer == "s5":
        y_ref, gmix_ref, d_ref, wglu_ref = next(it), next(it), next(it), next(it)
    else:
        q_ref, kc_ref, kp_ref, vc_ref, vp_ref, sink_ref, wo_ref = (next(it) for _ in range(7))
    gmlp_ref, wup_ref, wdown_ref, gple_ref, wgate_ref, wproj_ref = (next(it) for _ in range(6))
    if epilogue == "norm":
        gnext_ref = next(it)
    elif epilogue == "q":
        gnext_ref, wq_ref, inv_ref = next(it), next(it), next(it)
    elif epilogue == "kvq":
        gnext_ref, wq_ref, inv_ref, gkv_ref, wk_ref, wv_ref = (next(it) for _ in range(6))
    elif epilogue == "final":
        gnext_ref = next(it)
    ho_ref = None if epilogue == "final" else next(it)
    if epilogue == "norm":
        hn_ref = next(it)
    elif epilogue == "q":
        qo_ref = next(it)
    elif epilogue == "kvq":
        qo_ref, ko_ref, vo_ref = next(it), next(it), next(it)
    elif epilogue == "final":
        out_ref = next(it)
    if mixer == "attn":
        o_scr = next(it)

    h = h_ref[...]
    d_model = h.shape[-1]

    if mixer == "s5":
        u = _rms(h, gmix_ref[...])
        ge = _gelu_tanh(y_ref[...] + d_ref[...] * u).astype(BF16)
        ab = _dot(ge, wglu_ref[...])
        h = h + ab[:, :d_model] * jax.nn.sigmoid(ab[:, d_model:])
    else:
        _attention(q_ref, kc_ref, kp_ref, vc_ref, vp_ref, sink_ref, o_scr, tb=tb, seq=seq)
        h = h + _dot(o_scr[...], wo_ref[...])

    hm = _rms(h, gmlp_ref[...]).astype(BF16)
    acc = h
    d_ff = wup_ref.shape[1]
    for c0 in range(0, d_ff, FF_CHUNK):
        up = _dot(hm, wup_ref[:, c0:c0 + FF_CHUNK])
        act = jnp.square(jnp.maximum(up, 0.0)).astype(BF16)
        acc = acc + _dot(act, wdown_ref[c0:c0 + FF_CHUNK, :])
    h = acc

    gate = jax.nn.sigmoid(_dot(_rms(h, gple_ref[...]).astype(BF16), wgate_ref[...]))
    h = h + gate * _dot(p_ref[...].astype(BF16), wproj_ref[...])

    if epilogue == "final":
        out_ref[...] = _rms(h, gnext_ref[...])
        return
    ho_ref[...] = h
    if epilogue == "norm":
        hn_ref[...] = _rms(h, gnext_ref[...]).astype(hn_ref.dtype)
        return
    cos_t, sin_t, l64 = _rope_tables(inv_ref, tb, seq)
    hq = _rms(h, gnext_ref[...]).astype(BF16)
    q = _dot(hq, wq_ref[...])
    scale = HEAD_DIM ** -0.5
    for hp in range(d_model // LANES):
        tile = _rope(q[:, hp * LANES:(hp + 1) * LANES], cos_t, sin_t, l64)
        qo_ref[hp] = (tile * scale).astype(qo_ref.dtype)
    if epilogue == "kvq":
        hk = _rms(h, gkv_ref[...]).astype(BF16)
        k = _dot(hk, wk_ref[...])
        v = _dot(hk, wv_ref[...])
        for kh in range(N_KV_HEADS):
            ko_ref[kh] = _rope(k[:, kh * LANES:(kh + 1) * LANES], cos_t, sin_t, l64).astype(ko_ref.dtype)
            vo_ref[kh] = v[:, kh * LANES:(kh + 1) * LANES].astype(vo_ref.dtype)


def _layer_call(h, p, mixer_args, tail_args, epi_args, *, mixer, epilogue, seq):
    t, d = h.shape
    tb = min(TOKEN_BLOCK, seq)
    nsteps = t // tb
    row = lambda w: pl.BlockSpec((tb, w), lambda i: (i, 0))
    vec = lambda: pl.BlockSpec((1, d), lambda i: (0, 0))
    heads = lambda n: pl.BlockSpec((n, tb, LANES), lambda i: (0, i, 0))

    args = [h, p]
    specs = [row(d), row(p.shape[1])]
    if mixer == "s5":
        y, gmix, dskip, wglu = mixer_args
        args += [y, gmix, dskip, wglu]
        specs += [row(d), vec(), vec(), _resident(wglu.shape)]
    else:
        q, kd, vd, sinks, wo = mixer_args
        blocks_per_tb = tb // ATTN_BLOCK
        prev = lambda: pl.BlockSpec((N_KV_HEADS, ATTN_BLOCK, LANES),
                                    lambda i: (0, jnp.maximum(i * blocks_per_tb - 1, 0), 0))
        args += [q, kd, kd, vd, vd, sinks, wo]
        specs += [heads(d // LANES), heads(N_KV_HEADS), prev(), heads(N_KV_HEADS), prev(),
                  pl.BlockSpec(memory_space=pltpu.SMEM), _resident(wo.shape)]
    gmlp, wup, wdown, gple, wgate, wproj = tail_args
    args += [gmlp, wup, wdown, gple, wgate, wproj]
    specs += [vec(), _resident(wup.shape), _resident(wdown.shape), vec(),
              _resident(wgate.shape), _resident(wproj.shape)]

    inv_spec = pl.BlockSpec((1, LANES), lambda i: (0, 0))
    out_shapes, out_specs = [], []
    if epilogue != "final":
        out_shapes.append(jax.ShapeDtypeStruct((t, d), F32))
        out_specs.append(row(d))
    if epilogue == "norm":
        (gnext,) = epi_args
        args += [gnext]
        specs += [vec()]
        out_shapes.append(jax.ShapeDtypeStruct((t, d), F32))
        out_specs.append(row(d))
    elif epilogue == "q":
        gnext, wq, inv = epi_args
        args += [gnext, wq, inv]
        specs += [vec(), _resident(wq.shape), inv_spec]
        out_shapes.append(jax.ShapeDtypeStruct((d // LANES, t, LANES), BF16))
        out_specs.append(heads(d // LANES))
    elif epilogue == "kvq":
        gnext, wq, inv, gkv, wk, wv = epi_args
        args += [gnext, wq, inv, gkv, wk, wv]
        specs += [vec(), _resident(wq.shape), inv_spec, vec(), _resident(wk.shape), _resident(wv.shape)]
        out_shapes += [jax.ShapeDtypeStruct((d // LANES, t, LANES), BF16),
                       jax.ShapeDtypeStruct((N_KV_HEADS, t, LANES), BF16),
                       jax.ShapeDtypeStruct((N_KV_HEADS, t, LANES), BF16)]
        out_specs += [heads(d // LANES), heads(N_KV_HEADS), heads(N_KV_HEADS)]
    elif epilogue == "final":
        (gnext,) = epi_args
        args += [gnext]
        specs += [vec()]
        out_shapes.append(jax.ShapeDtypeStruct((t, d), F32))
        out_specs.append(row(d))

    scratch = [pltpu.VMEM((tb, d), BF16)] if mixer == "attn" else []
    kern = functools.partial(_layer_kernel, mixer=mixer, epilogue=epilogue, tb=tb, seq=seq)
    return pl.pallas_call(
        kern,
        grid=(nsteps,),
        in_specs=specs,
        out_specs=out_specs,
        out_shape=out_shapes,
        scratch_shapes=scratch,
        compiler_params=pltpu.CompilerParams(
            dimension_semantics=("arbitrary",), vmem_limit_bytes=VMEM_LIMIT_BYTES),
        name=f"layer_{mixer}_{epilogue}",
    )(*args)


def kernel(x, p, norm_mix, ssm_lambda_re, ssm_lambda_im, ssm_log_dt, ssm_b_re, ssm_b_im, ssm_c_re, ssm_c_im, ssm_d, ssm_w_glu, kv_norm, w_k, w_v, w_q, attn_sinks, w_o, norm_mlp, w_up, w_down, norm_ple, w_ple_gate, w_ple_proj, norm_final):
    bsz, seq, d = x.shape
    depth = p.shape[0]
    n_a = ssm_lambda_re.shape[0]
    groups = ssm_lambda_re.shape[1]
    t = bsz * seq
    L = S5_CHUNK
    m = t // L
    assert seq % (L * SUBLANES) == 0 and seq % ATTN_BLOCK == 0 and d % LANES == 0

    vec = lambda g: g.reshape(1, d)
    inv = ROPE_THETA ** (-jnp.arange(0, ROT_DIM, 2, dtype=F32) / ROT_DIM)
    inv64 = jnp.concatenate([inv, inv, jnp.zeros((HEAD_DIM - ROT_DIM,), F32)])
    inv_lanes = jnp.tile(inv64, LANES // HEAD_DIM).reshape(1, LANES)
    dup = lambda w: jnp.repeat(w.reshape(d, N_KV_HEADS, 1, HEAD_DIM), 2, axis=2).reshape(d, 2 * N_KV_HEADS * HEAD_DIM)

    h = x.reshape(t, d)
    hn = _prenorm(h, vec(norm_mix[0]))
    q = kd = vd = None
    out = None
    for i in range(depth):
        p_i = p[i].reshape(t, p.shape[-1])
        tail = (vec(norm_mlp[i]), w_up[i].astype(BF16), w_down[i].astype(BF16),
                vec(norm_ple[i]), w_ple_gate[i].astype(BF16), w_ple_proj[i].astype(BF16))
        last = i == depth - 1
        if last:
            epilogue, epi = "final", (vec(norm_final),)
        elif i + 1 < n_a:
            epilogue, epi = "norm", (vec(norm_mix[i + 1]),)
        elif i + 1 == n_a:
            epilogue = "kvq"
            epi = (vec(norm_mix[i + 1]), w_q[0].astype(BF16), inv_lanes,
                   vec(kv_norm), dup(w_k).astype(BF16), dup(w_v).astype(BF16))
        else:
            epilogue, epi = "q", (vec(norm_mix[i + 1]), w_q[i + 1 - n_a].astype(BF16), inv_lanes)

        if i < n_a:
            tt, pb, qb, sc = _s5_tables(ssm_lambda_re[i], ssm_lambda_im[i], ssm_log_dt[i],
                                        ssm_b_re[i], ssm_b_im[i], ssm_c_re[i], ssm_c_im[i])
            y = _s5_core(hn, tt, pb, qb, sc, chunks_per_seq=seq // L)
            mixer, mix = "s5", (y, vec(norm_mix[i]), vec(ssm_d[i]), ssm_w_glu[i].astype(BF16))
        else:
            j = i - n_a
            mixer, mix = "attn", (q, kd, vd, attn_sinks[j], w_o[j].astype(BF16))

        res = _layer_call(h, p_i, mix, tail, epi, mixer=mixer, epilogue=epilogue, seq=seq)
        if epilogue == "final":
            out = res[0]
        elif epilogue == "norm":
            h, hn = res
        elif epilogue == "q":
            h, q = res
        else:
            h, q, kd, vd = res
    return out.reshape(bsz, seq, d)
```

```python
import functools
import math

import jax
import jax.numpy as jnp
import numpy as np
from jax import lax
from jax.experimental import pallas as pl
from jax.experimental.pallas import tpu as pltpu

SSM_GROUP = 16
SSM_STATE = 64
HEAD_DIM = 64
N_KV_HEADS = 4
GQA_GROUP = 4
ATTN_BLOCK = 128
WINDOW = 128
ROPE_THETA = 500000.0
ROT_DIM = 16
RMS_EPS = 1e-6
NEG_INF = -1e30

LANES = 128
SUBLANES = 8
VMEM_LIMIT_BYTES = 56 * 1024 * 1024

S5_CHUNK = 16
S5_ROWS = 256
TOKEN_BLOCK = 512
SUB_BLOCK = 256
FF_CHUNK = 1024

BF16 = jnp.bfloat16
F32 = jnp.float32


def _dot(a, b):
    return jnp.dot(a, b, preferred_element_type=F32)


def _rms(x, g):
    return x * lax.rsqrt(jnp.mean(x * x, axis=-1, keepdims=True) + RMS_EPS) * g


def _gelu_tanh(x):
    c = math.sqrt(2.0 / math.pi)
    return 0.5 * x * (1.0 + jnp.tanh(c * (x + 0.044715 * (x * x * x))))


def _resident(shape):
    nd = len(shape)
    return pl.BlockSpec(shape, lambda i: (0,) * nd, pipeline_mode=pl.Buffered(1))


def _s5_tables(lam_re, lam_im, log_dt, b_re, b_im, c_re, c_im):
    hp = lax.Precision.HIGHEST
    L = S5_CHUNK
    G, N = lam_re.shape
    H = b_re.shape[-1]
    dt = jnp.exp(log_dt)[:, None]
    lr, ph = lam_re * dt, lam_im * dt
    d = jnp.arange(L + 1, dtype=F32)[:, None, None]
    pw_r = jnp.exp(d * lr) * jnp.cos(d * ph)
    pw_i = jnp.exp(d * lr) * jnp.sin(d * ph)
    a_r, a_i = pw_r[1], pw_i[1]
    den = lam_re * lam_re + lam_im * lam_im
    nr = a_r - 1.0
    coef_r = (nr * lam_re + a_i * lam_im) / den
    coef_i = (a_i * lam_re - nr * lam_im) / den
    bb_r = coef_r[..., None] * b_re - coef_i[..., None] * b_im
    bb_i = coef_r[..., None] * b_im + coef_i[..., None] * b_re
    ab_r = pw_r[:L, :, :, None] * bb_r - pw_i[:L, :, :, None] * bb_i
    ab_i = pw_r[:L, :, :, None] * bb_i + pw_i[:L, :, :, None] * bb_r
    kc = (jnp.einsum('dgnh,gkn->ghdk', ab_r, c_re, precision=hp)
          - jnp.einsum('dgnh,gkn->ghdk', ab_i, c_im, precision=hp)).reshape(G, H, L * H)

    p_r = jnp.transpose(ab_r[::-1], (1, 0, 3, 2)).reshape(G // 2, 2, L * H, N)
    p_i = jnp.transpose(ab_i[::-1], (1, 0, 3, 2)).reshape(G // 2, 2, L * H, N)
    zp = jnp.zeros_like(p_r[:, 0])
    pb = jnp.concatenate([jnp.concatenate([p_r[:, 0], zp, p_i[:, 0], zp], axis=-1),
                          jnp.concatenate([zp, p_r[:, 1], zp, p_i[:, 1]], axis=-1)], axis=1)

    ca_r = c_re[None] * pw_r[1:, :, None, :] - c_im[None] * pw_i[1:, :, None, :]
    ca_i = c_re[None] * pw_i[1:, :, None, :] + c_im[None] * pw_r[1:, :, None, :]
    q_r = jnp.transpose(ca_r, (1, 3, 0, 2)).reshape(G // 2, 2, N, L * H)
    q_i = jnp.transpose(-ca_i, (1, 3, 0, 2)).reshape(G // 2, 2, N, L * H)
    zq = jnp.zeros_like(q_r[:, 0])
    qb = jnp.concatenate([jnp.concatenate([q_r[:, 0], zq], axis=-1), jnp.concatenate([zq, q_r[:, 1]], axis=-1),
                          jnp.concatenate([q_i[:, 0], zq], axis=-1), jnp.concatenate([zq, q_i[:, 1]], axis=-1)],
                         axis=1)

    mm = jnp.array([1, 2, 3, 4, 5, 6, 7, 8, 1, 2, 4, 8, 0, 0, 0, 0], F32)[:, None, None] * L
    sc = jnp.stack([jnp.exp(mm * lr) * jnp.cos(mm * ph), jnp.exp(mm * lr) * jnp.sin(mm * ph)])
    sc = jnp.transpose(sc.reshape(2, 16, G // 2, 2 * N), (2, 0, 1, 3))
    return kc, pb.astype(BF16), qb.astype(BF16), sc


def _block_transpose(vs, lane_blk):
    vs = list(vs)
    for d in (4, 2, 1):
        take = (lane_blk & d) != 0
        for r in range(len(vs)):
            if r & d:
                continue
            top, bot = vs[r], vs[r + d]
            vs[r] = jnp.where(take, pltpu.roll(bot, d * SSM_GROUP, axis=1), top)
            vs[r + d] = jnp.where(take, bot, pltpu.roll(top, LANES - d * SSM_GROUP, axis=1))
    return vs


def _toeplitz_rows(kc, j, lane):
    lo, hi = kc[:, :LANES], kc[:, LANES:]
    r = (j % SUBLANES) * SSM_GROUP
    if r:
        lo_s, hi_s = pltpu.roll(lo, r, axis=1), pltpu.roll(hi, r, axis=1)
        first = jnp.where(lane < r, 0.0, lo_s)
        second = jnp.where(lane < r, lo_s, hi_s)
    else:
        first, second = lo, hi
    if j < SUBLANES:
        return jnp.concatenate([first, second], axis=1)
    return jnp.concatenate([jnp.zeros_like(first), first], axis=1)


def _s5_core_kernel(hn_ref, kc_ref, p_ref, q_ref, sc_ref, y_ref, t_scr, u_scr, v_scr, x_scr, yg_scr, carry_scr,
                    *, rows, tiles_per_seq):
    L = S5_CHUNK
    n_pair = p_ref.shape[0]
    kk = t_scr.shape[1]
    rb = pl.program_id(1)
    lane_blk = lax.broadcasted_iota(jnp.int32, (rows, LANES), 1) // SSM_GROUP

    @pl.when(rb == 0)
    def _():
        carry_scr[...] = jnp.zeros_like(carry_scr)
        lane = lax.broadcasted_iota(jnp.int32, (SSM_GROUP, LANES), 1)
        for g in range(2 * n_pair):
            kc = kc_ref[g]
            for j in range(L):
                t_scr[g, j * SSM_GROUP:(j + 1) * SSM_GROUP, :] = _toeplitz_rows(kc, j, lane).astype(BF16)

    for jh in range(L // SUBLANES):
        vs = [hn_ref[pl.ds(jh * SUBLANES + r, rows, stride=L), :] for r in range(SUBLANES)]
        vs = _block_transpose(vs, lane_blk)
        for gl in range(2 * n_pair):
            col = ((gl % 2) * (L // SUBLANES) + jh) * LANES
            u_scr[gl // 2, :, col:col + LANES] = vs[gl].astype(BF16)

    for gp in range(n_pair):
        v_scr[gp] = _dot(u_scr[gp], p_ref[gp])

    half = LANES
    row = lax.broadcasted_iota(jnp.int32, (SUBLANES, half), 0)
    tiles = rows // SUBLANES

    def tile_body(t, carry):
        keep = ((rb * tiles + t) % tiles_per_seq) != 0
        r0 = pl.multiple_of(t * SUBLANES, SUBLANES)
        out = []
        for gp in range(n_pair):
            cr = jnp.where(keep, carry[2 * gp], 0.0)
            ci = jnp.where(keep, carry[2 * gp + 1], 0.0)
            r = v_scr[gp, pl.ds(r0, SUBLANES), 0:half]
            i = v_scr[gp, pl.ds(r0, SUBLANES), half:2 * half]
            for s_idx, s in enumerate((1, 2, 4)):
                ar = sc_ref[gp, 0, 8 + s_idx:9 + s_idx, :]
                ai = sc_ref[gp, 1, 8 + s_idx:9 + s_idx, :]
                rs = jnp.where(row >= s, pltpu.roll(r, s, axis=0), 0.0)
                is_ = jnp.where(row >= s, pltpu.roll(i, s, axis=0), 0.0)
                r, i = r + (ar * rs - ai * is_), i + (ar * is_ + ai * rs)
            pw_r = sc_ref[gp, 0, 0:SUBLANES, :]
            pw_i = sc_ref[gp, 1, 0:SUBLANES, :]
            inc_r = r + (pw_r * cr - pw_i * ci)
            inc_i = i + (pw_r * ci + pw_i * cr)
            x_scr[gp, pl.ds(r0, SUBLANES), 0:half] = jnp.where(row == 0, cr, pltpu.roll(inc_r, 1, axis=0))
            x_scr[gp, pl.ds(r0, SUBLANES), half:2 * half] = jnp.where(row == 0, ci, pltpu.roll(inc_i, 1, axis=0))
            last_r = jnp.broadcast_to(r[SUBLANES - 1:SUBLANES, :], (SUBLANES, half))
            last_i = jnp.broadcast_to(i[SUBLANES - 1:SUBLANES, :], (SUBLANES, half))
            a8r = sc_ref[gp, 0, 11:12, :]
            a8i = sc_ref[gp, 1, 11:12, :]
            out += [last_r + (a8r * cr - a8i * ci), last_i + (a8r * ci + a8i * cr)]
        return tuple(out)

    init = tuple(carry_scr[k] for k in range(2 * n_pair))
    fin = lax.fori_loop(0, tiles, tile_body, init, unroll=2)
    for k in range(2 * n_pair):
        carry_scr[k] = fin[k]

    for gp in range(n_pair):
        carry_y = _dot(x_scr[gp].astype(BF16), q_ref[gp])
        for g2 in range(2):
            cols = slice(g2 * kk, (g2 + 1) * kk)
            yg_scr[2 * gp + g2] = _dot(u_scr[gp, :, cols], t_scr[2 * gp + g2]) + carry_y[:, cols]

    for ih in range(L // SUBLANES):
        vs = [yg_scr[gl, :, ih * LANES:(ih + 1) * LANES] for gl in range(2 * n_pair)]
        vs = _block_transpose(vs, lane_blk)
        for r in range(SUBLANES):
            y_ref[pl.ds(ih * SUBLANES + r, rows, stride=L), :] = vs[r]


def _s5_core(hn, kc, pb, qb, sc, *, chunks_per_seq):
    tokens, d = hn.shape
    L = S5_CHUNK
    m = tokens // L
    rows = min(S5_ROWS, m)
    kk = L * SSM_GROUP
    n_pair = LANES // (2 * SSM_GROUP)
    kern = functools.partial(_s5_core_kernel, rows=rows, tiles_per_seq=chunks_per_seq // SUBLANES)
    per_tile = lambda *shape: pl.BlockSpec(shape, lambda lt, rb: (lt,) + (0,) * (len(shape) - 1))
    return pl.pallas_call(
        kern,
        grid=(d // LANES, m // rows),
        in_specs=[
            pl.BlockSpec((rows * L, LANES), lambda lt, rb: (rb, lt)),
            per_tile(2 * n_pair, SSM_GROUP, kk),
            per_tile(n_pair, 2 * kk, kk),
            per_tile(n_pair, kk, 2 * kk),
            per_tile(n_pair, 2, 16, LANES),
        ],
        out_specs=pl.BlockSpec((rows * L, LANES), lambda lt, rb: (rb, lt)),
        out_shape=jax.ShapeDtypeStruct((tokens, d), F32),
        scratch_shapes=[
            pltpu.VMEM((2 * n_pair, kk, kk), BF16),
            pltpu.VMEM((n_pair, rows, 2 * kk), BF16),
            pltpu.VMEM((n_pair, rows, 2 * LANES), F32),
            pltpu.VMEM((n_pair, rows, 2 * LANES), F32),
            pltpu.VMEM((2 * n_pair, rows, kk), F32),
            pltpu.VMEM((2 * n_pair, SUBLANES, LANES), F32),
        ],
        compiler_params=pltpu.CompilerParams(
            dimension_semantics=("arbitrary", "arbitrary"), vmem_limit_bytes=VMEM_LIMIT_BYTES),
        name="s5_core",
    )(hn, kc, pb, qb, sc)


def _prenorm_kernel(x_ref, g_ref, o_ref):
    o_ref[...] = _rms(x_ref[...], g_ref[...]).astype(o_ref.dtype)


def _prenorm(x, g):
    t, d = x.shape
    tb = min(t, 1024)
    return pl.pallas_call(
        _prenorm_kernel,
        grid=(t // tb,),
        in_specs=[pl.BlockSpec((tb, d), lambda i: (i, 0)), pl.BlockSpec((1, d), lambda i: (0, 0))],
        out_specs=pl.BlockSpec((tb, d), lambda i: (i, 0)),
        out_shape=jax.ShapeDtypeStruct((t, d), F32),
        compiler_params=pltpu.CompilerParams(dimension_semantics=("arbitrary",)),
        name="prenorm",
    )(x, g)


def _rope(x, cos_t, sin_t, l64):
    half = ROT_DIM // 2
    lo = pltpu.roll(x, half, axis=1)
    hi = pltpu.roll(x, LANES - half, axis=1)
    return x * cos_t + jnp.where(l64 < half, -hi, lo) * sin_t


def _rope_tables(inv_ref, row0, nrows, seq):
    pos = (row0 + lax.broadcasted_iota(jnp.int32, (nrows, LANES), 0)) % seq
    ang = pos.astype(F32) * inv_ref[...]
    l64 = lax.broadcasted_iota(jnp.int32, (nrows, LANES), 1) % HEAD_DIM
    return jnp.cos(ang), jnp.sin(ang), l64


def _attention_consts():
    nq = ATTN_BLOCK
    ii = lax.broadcasted_iota(jnp.int32, (nq, 2 * nq), 0)
    jj = lax.broadcasted_iota(jnp.int32, (nq, 2 * nq), 1) % nq
    cur = jj <= ii
    lane = lax.broadcasted_iota(jnp.int32, (nq, LANES), 1)
    first = lane < HEAD_DIM
    return dict(ii=ii, jj=jj, cur=cur, first=first, zero=jnp.zeros((nq, LANES), BF16),
                cur_bf=jnp.where(cur, 1.0, 0.0).astype(BF16), prev_bf=jnp.where(cur, 0.0, 1.0).astype(BF16),
                ones_a=jnp.where(first, 1.0, 0.0).astype(BF16), ones_b=jnp.where(first, 0.0, 1.0).astype(BF16))


def _kv_blocks(k, v, c):
    first, zero = c["first"], c["zero"]
    k_blk = jnp.concatenate([jnp.where(first, k, zero), jnp.where(first, zero, k)], axis=0)
    v_blk = jnp.concatenate([
        jnp.concatenate([jnp.where(first, v, zero), c["ones_a"]], axis=1),
        jnp.concatenate([jnp.where(first, zero, v), c["ones_b"]], axis=1)], axis=0)
    return k_blk, v_blk


def _attention(q_ref, kc_ref, kp_ref, vc_ref, vp_ref, sink_ref, o_scr, c, *, row0, nrows, tb, seq):
    nq = ATTN_BLOCK
    nt = (((1,), (1,)), ((), ()))
    blocks = {}

    def kv(kh, b):
        if (kh, b) not in blocks:
            if b < 0:
                blocks[kh, b] = _kv_blocks(kp_ref[kh], vp_ref[kh], c)
            else:
                rows = slice(b * nq, (b + 1) * nq)
                blocks[kh, b] = _kv_blocks(kc_ref[kh, rows, :], vc_ref[kh, rows, :], c)
        return blocks[kh, b]

    for b in range(row0 // nq, (row0 + nrows) // nq):
        rows = slice(b * nq, (b + 1) * nq)
        has_prev = ((pl.program_id(0) * (tb // nq) + b) % (seq // nq)) != 0
        valid = c["jj"] <= c["ii"] + jnp.where(has_prev, nq, 0)
        for kh in range(N_KV_HEADS):
            k_cur, v_cur = kv(kh, b)
            k_prev, v_prev = kv(kh, b - 1)
            for pair in range(GQA_GROUP // 2):
                hp = kh * (GQA_GROUP // 2) + pair
                qp = q_ref[hp, rows, :]
                s_cur = lax.dot_general(qp, k_cur, nt, preferred_element_type=F32)
                s_prev = lax.dot_general(qp, k_prev, nt, preferred_element_type=F32)
                s = jnp.where(valid, jnp.where(c["cur"], s_cur, s_prev), NEG_INF)
                es, sinks = [], []
                for hh in range(2):
                    sh = s[:, hh * nq:(hh + 1) * nq]
                    sink = sink_ref[2 * hp + hh]
                    mx = jnp.maximum(jnp.max(sh, axis=-1, keepdims=True), sink)
                    es.append(jnp.exp(sh - mx))
                    sinks.append(jnp.exp(sink - mx))
                e = jnp.concatenate(es, axis=1).astype(BF16)
                o = _dot(e * c["cur_bf"], v_cur) + _dot(e * c["prev_bf"], v_prev)
                den = o[:, LANES:] + jnp.where(c["first"], sinks[0], sinks[1])
                o_scr[rows, hp * LANES:(hp + 1) * LANES] = (o[:, :LANES] / den).astype(o_scr.dtype)


def _layer_kernel(*refs, mixer, epilogue, tb, seq):
    it = iter(refs)
    h_ref = next(it)
    p_ref = next(it)
    if mixer == "s5":
        y_ref, gmix_ref, d_ref, wglu_ref = next(it), next(it), next(it), next(it)
    else:
        q_ref, kc_ref, kp_ref, vc_ref, vp_ref, sink_ref, wo_ref = (next(it) for _ in range(7))
    gmlp_ref, wup_ref, wdown_ref, gple_ref, wgate_ref, wproj_ref = (next(it) for _ in range(6))
    if epilogue == "norm":
        gnext_ref = next(it)
    elif epilogue == "q":
        gnext_ref, wq_ref, inv_ref = next(it), next(it), next(it)
    elif epilogue == "kvq":
        gnext_ref, wq_ref, inv_ref, gkv_ref, wk_ref, wv_ref = (next(it) for _ in range(6))
    elif epilogue == "final":
        gnext_ref = next(it)
    ho_ref = None if epilogue == "final" else next(it)
    if epilogue == "norm":
        hn_ref = next(it)
    elif epilogue == "q":
        qo_ref = next(it)
    elif epilogue == "kvq":
        qo_ref, ko_ref, vo_ref = next(it), next(it), next(it)
    elif epilogue == "final":
        out_ref = next(it)
    if mixer == "attn":
        o_scr = next(it)

    d_model = h_ref.shape[-1]
    d_ff = wup_ref.shape[1]
    sub = min(SUB_BLOCK, tb)
    consts = _attention_consts() if mixer == "attn" else None

    for r0 in range(0, tb, sub):
        rows = slice(r0, r0 + sub)
        h = h_ref[rows, :]

        if mixer == "s5":
            u = _rms(h, gmix_ref[...])
            ge = _gelu_tanh(y_ref[rows, :] + d_ref[...] * u).astype(BF16)
            ab = _dot(ge, wglu_ref[...])
            h = h + ab[:, :d_model] * jax.nn.sigmoid(ab[:, d_model:])
        else:
            _attention(q_ref, kc_ref, kp_ref, vc_ref, vp_ref, sink_ref, o_scr, consts,
                       row0=r0, nrows=sub, tb=tb, seq=seq)
            h = h + _dot(o_scr[rows, :], wo_ref[...])

        hm = _rms(h, gmlp_ref[...]).astype(BF16)
        acc = h
        for c0 in range(0, d_ff, FF_CHUNK):
            up = _dot(hm, wup_ref[:, c0:c0 + FF_CHUNK])
            act = jnp.square(jnp.maximum(up, 0.0)).astype(BF16)
            acc = acc + _dot(act, wdown_ref[c0:c0 + FF_CHUNK, :])
        h = acc

        gate = jax.nn.sigmoid(_dot(_rms(h, gple_ref[...]).astype(BF16), wgate_ref[...]))
        h = h + gate * _dot(p_ref[rows, :].astype(BF16), wproj_ref[...])

        if epilogue == "final":
            out_ref[rows, :] = _rms(h, gnext_ref[...])
            continue
        ho_ref[rows, :] = h
        if epilogue == "norm":
            hn_ref[rows, :] = _rms(h, gnext_ref[...]).astype(hn_ref.dtype)
            continue
        cos_t, sin_t, l64 = _rope_tables(inv_ref, pl.program_id(0) * tb + r0, sub, seq)
        hq = _rms(h, gnext_ref[...]).astype(BF16)
        q = _dot(hq, wq_ref[...])
        scale = HEAD_DIM ** -0.5
        for hp in range(d_model // LANES):
            tile = _rope(q[:, hp * LANES:(hp + 1) * LANES], cos_t, sin_t, l64)
            qo_ref[hp, rows, :] = (tile * scale).astype(qo_ref.dtype)
        if epilogue == "kvq":
            hk = _rms(h, gkv_ref[...]).astype(BF16)
            k = _dot(hk, wk_ref[...])
            v = _dot(hk, wv_ref[...])
            for kh in range(N_KV_HEADS):
                tile = _rope(k[:, kh * LANES:(kh + 1) * LANES], cos_t, sin_t, l64)
                ko_ref[kh, rows, :] = tile.astype(ko_ref.dtype)
                vo_ref[kh, rows, :] = v[:, kh * LANES:(kh + 1) * LANES].astype(vo_ref.dtype)


def _layer_call(h, p, mixer_args, tail_args, epi_args, *, mixer, epilogue, seq):
    t, d = h.shape
    tb = min(TOKEN_BLOCK, seq)
    nsteps = t // tb
    row = lambda w: pl.BlockSpec((tb, w), lambda i: (i, 0))
    vec = lambda: pl.BlockSpec((1, d), lambda i: (0, 0))
    heads = lambda n: pl.BlockSpec((n, tb, LANES), lambda i: (0, i, 0))

    args = [h, p]
    specs = [row(d), row(p.shape[1])]
    if mixer == "s5":
        y, gmix, dskip, wglu = mixer_args
        args += [y, gmix, dskip, wglu]
        specs += [row(d), vec(), vec(), _resident(wglu.shape)]
    else:
        q, kd, vd, sinks, wo = mixer_args
        blocks_per_tb = tb // ATTN_BLOCK
        prev = lambda: pl.BlockSpec((N_KV_HEADS, ATTN_BLOCK, LANES),
                                    lambda i: (0, jnp.maximum(i * blocks_per_tb - 1, 0), 0))
        args += [q, kd, kd, vd, vd, sinks, wo]
        specs += [heads(d // LANES), heads(N_KV_HEADS), prev(), heads(N_KV_HEADS), prev(),
                  pl.BlockSpec(memory_space=pltpu.SMEM), _resident(wo.shape)]
    gmlp, wup, wdown, gple, wgate, wproj = tail_args
    args += [gmlp, wup, wdown, gple, wgate, wproj]
    specs += [vec(), _resident(wup.shape), _resident(wdown.shape), vec(),
              _resident(wgate.shape), _resident(wproj.shape)]

    inv_spec = pl.BlockSpec((1, LANES), lambda i: (0, 0))
    out_shapes, out_specs = [], []
    if epilogue != "final":
        out_shapes.append(jax.ShapeDtypeStruct((t, d), F32))
        out_specs.append(row(d))
    if epilogue == "norm":
        (gnext,) = epi_args
        args += [gnext]
        specs += [vec()]
        out_shapes.append(jax.ShapeDtypeStruct((t, d), F32))
        out_specs.append(row(d))
    elif epilogue == "q":
        gnext, wq, inv = epi_args
        args += [gnext, wq, inv]
        specs += [vec(), _resident(wq.shape), inv_spec]
        out_shapes.append(jax.ShapeDtypeStruct((d // LANES, t, LANES), BF16))
        out_specs.append(heads(d // LANES))
    elif epilogue == "kvq":
        gnext, wq, inv, gkv, wk, wv = epi_args
        args += [gnext, wq, inv, gkv, wk, wv]
        specs += [vec(), _resident(wq.shape), inv_spec, vec(), _resident(wk.shape), _resident(wv.shape)]
        out_shapes += [jax.ShapeDtypeStruct((d // LANES, t, LANES), BF16),
                       jax.ShapeDtypeStruct((N_KV_HEADS, t, LANES), BF16),
                       jax.ShapeDtypeStruct((N_KV_HEADS, t, LANES), BF16)]
        out_specs += [heads(d // LANES), heads(N_KV_HEADS), heads(N_KV_HEADS)]
    elif epilogue == "final":
        (gnext,) = epi_args
        args += [gnext]
        specs += [vec()]
        out_shapes.append(jax.ShapeDtypeStruct((t, d), F32))
        out_specs.append(row(d))

    scratch = [pltpu.VMEM((tb, d), BF16)] if mixer == "attn" else []
    kern = functools.partial(_layer_kernel, mixer=mixer, epilogue=epilogue, tb=tb, seq=seq)
    return pl.pallas_call(
        kern,
        grid=(nsteps,),
        in_specs=specs,
        out_specs=out_specs,
        out_shape=out_shapes,
        scratch_shapes=scratch,
        compiler_params=pltpu.CompilerParams(
            dimension_semantics=("arbitrary",), vmem_limit_bytes=VMEM_LIMIT_BYTES),
        name=f"layer_{mixer}_{epilogue}",
    )(*args)


def kernel(x, p, norm_mix, ssm_lambda_re, ssm_lambda_im, ssm_log_dt, ssm_b_re, ssm_b_im, ssm_c_re, ssm_c_im, ssm_d, ssm_w_glu, kv_norm, w_k, w_v, w_q, attn_sinks, w_o, norm_mlp, w_up, w_down, norm_ple, w_ple_gate, w_ple_proj, norm_final):
    bsz, seq, d = x.shape
    depth = p.shape[0]
    n_a = ssm_lambda_re.shape[0]
    groups = ssm_lambda_re.shape[1]
    t = bsz * seq
    L = S5_CHUNK
    m = t // L
    assert seq % (L * SUBLANES) == 0 and seq % ATTN_BLOCK == 0 and d % LANES == 0

    vec = lambda g: g.reshape(1, d)
    inv = ROPE_THETA ** (-jnp.arange(0, ROT_DIM, 2, dtype=F32) / ROT_DIM)
    inv64 = jnp.concatenate([inv, inv, jnp.zeros((HEAD_DIM - ROT_DIM,), F32)])
    inv_lanes = jnp.tile(inv64, LANES // HEAD_DIM).reshape(1, LANES)
    dup = lambda w: jnp.repeat(w.reshape(d, N_KV_HEADS, 1, HEAD_DIM), 2, axis=2).reshape(d, 2 * N_KV_HEADS * HEAD_DIM)

    h = x.reshape(t, d)
    hn = _prenorm(h, vec(norm_mix[0]))
    q = kd = vd = None
    out = None
    for i in range(depth):
        p_i = p[i].reshape(t, p.shape[-1])
        tail = (vec(norm_mlp[i]), w_up[i].astype(BF16), w_down[i].astype(BF16),
                vec(norm_ple[i]), w_ple_gate[i].astype(BF16), w_ple_proj[i].astype(BF16))
        last = i == depth - 1
        if last:
            epilogue, epi = "final", (vec(norm_final),)
        elif i + 1 < n_a:
            epilogue, epi = "norm", (vec(norm_mix[i + 1]),)
        elif i + 1 == n_a:
            epilogue = "kvq"
            epi = (vec(norm_mix[i + 1]), w_q[0].astype(BF16), inv_lanes,
                   vec(kv_norm), dup(w_k).astype(BF16), dup(w_v).astype(BF16))
        else:
            epilogue, epi = "q", (vec(norm_mix[i + 1]), w_q[i + 1 - n_a].astype(BF16), inv_lanes)

        if i < n_a:
            tt, pb, qb, sc = _s5_tables(ssm_lambda_re[i], ssm_lambda_im[i], ssm_log_dt[i],
                                        ssm_b_re[i], ssm_b_im[i], ssm_c_re[i], ssm_c_im[i])
            y = _s5_core(hn, tt, pb, qb, sc, chunks_per_seq=seq // L)
            mixer, mix = "s5", (y, vec(norm_mix[i]), vec(ssm_d[i]), ssm_w_glu[i].astype(BF16))
        else:
            j = i - n_a
            mixer, mix = "attn", (q, kd, vd, attn_sinks[j], w_o[j].astype(BF16))

        res = _layer_call(h, p_i, mix, tail, epi, mixer=mixer, epilogue=epilogue, seq=seq)
        if epilogue == "final":
            out = res[0]
        elif epilogue == "norm":
            h, hn = res
        elif epilogue == "q":
            h, q = res
        else:
            h, q, kd, vd = res
    return out.reshape(bsz, seq, d)
```

```python
import functools
import math

import jax
import jax.numpy as jnp
from jax import lax
from jax.experimental import pallas as pl
from jax.experimental.pallas import tpu as pltpu

SSM_GROUP = 16
SSM_STATE = 64
HEAD_DIM = 64
N_KV_HEADS = 4
GQA_GROUP = 4
ATTN_BLOCK = 128
WINDOW = 128
ROPE_THETA = 500000.0
ROT_DIM = 16
RMS_EPS = 1e-6
NEG_INF = -1e30

LANES = 128
SUBLANES = 8
VMEM_LIMIT_BYTES = 56 * 1024 * 1024

S5_CHUNK = 16
S5_ROWS = 256
TOKEN_BLOCK = 512
SUB_BLOCK = 256
FF_CHUNK = 1024

BF16 = jnp.bfloat16
F32 = jnp.float32


def _dot(a, b):
    return jnp.dot(a, b, preferred_element_type=F32)


def _rms(x, g):
    return x * lax.rsqrt(jnp.mean(x * x, axis=-1, keepdims=True) + RMS_EPS) * g


def _gelu_tanh(x):
    c = math.sqrt(2.0 / math.pi)
    return 0.5 * x * (1.0 + jnp.tanh(c * (x + 0.044715 * (x * x * x))))


def _resident(shape):
    nd = len(shape)
    return pl.BlockSpec(shape, lambda i: (0,) * nd, pipeline_mode=pl.Buffered(1))


def _s5_tables(lam_re, lam_im, log_dt, b_re, b_im, c_re, c_im):
    hp = lax.Precision.HIGHEST
    L = S5_CHUNK
    G, N = lam_re.shape
    H = b_re.shape[-1]
    dt = jnp.exp(log_dt)[:, None]
    lr, ph = lam_re * dt, lam_im * dt
    d = jnp.arange(L + 1, dtype=F32)[:, None, None]
    pw_r = jnp.exp(d * lr) * jnp.cos(d * ph)
    pw_i = jnp.exp(d * lr) * jnp.sin(d * ph)
    a_r, a_i = pw_r[1], pw_i[1]
    den = lam_re * lam_re + lam_im * lam_im
    nr = a_r - 1.0
    coef_r = (nr * lam_re + a_i * lam_im) / den
    coef_i = (a_i * lam_re - nr * lam_im) / den
    bb_r = coef_r[..., None] * b_re - coef_i[..., None] * b_im
    bb_i = coef_r[..., None] * b_im + coef_i[..., None] * b_re
    ab_r = pw_r[:L, :, :, None] * bb_r - pw_i[:L, :, :, None] * bb_i
    ab_i = pw_r[:L, :, :, None] * bb_i + pw_i[:L, :, :, None] * bb_r
    kc = (jnp.einsum('dgnh,gkn->ghdk', ab_r, c_re, precision=hp)
          - jnp.einsum('dgnh,gkn->ghdk', ab_i, c_im, precision=hp)).reshape(G, H, L * H)

    p_r = jnp.transpose(ab_r[::-1], (1, 0, 3, 2)).reshape(G // 2, 2, L * H, N)
    p_i = jnp.transpose(ab_i[::-1], (1, 0, 3, 2)).reshape(G // 2, 2, L * H, N)
    zp = jnp.zeros_like(p_r[:, 0])
    pb = jnp.concatenate([jnp.concatenate([p_r[:, 0], zp, p_i[:, 0], zp], axis=-1),
                          jnp.concatenate([zp, p_r[:, 1], zp, p_i[:, 1]], axis=-1)], axis=1)

    ca_r = c_re[None] * pw_r[1:, :, None, :] - c_im[None] * pw_i[1:, :, None, :]
    ca_i = c_re[None] * pw_i[1:, :, None, :] + c_im[None] * pw_r[1:, :, None, :]
    q_r = jnp.transpose(ca_r, (1, 3, 0, 2)).reshape(G // 2, 2, N, L * H)
    q_i = jnp.transpose(-ca_i, (1, 3, 0, 2)).reshape(G // 2, 2, N, L * H)
    zq = jnp.zeros_like(q_r[:, 0])
    qb = jnp.concatenate([jnp.concatenate([q_r[:, 0], zq], axis=-1), jnp.concatenate([zq, q_r[:, 1]], axis=-1),
                          jnp.concatenate([q_i[:, 0], zq], axis=-1), jnp.concatenate([zq, q_i[:, 1]], axis=-1)],
                         axis=1)

    mm = jnp.array([1, 2, 3, 4, 5, 6, 7, 8, 1, 2, 4, 8, 0, 0, 0, 0], F32)[:, None, None] * L
    sc = jnp.stack([jnp.exp(mm * lr) * jnp.cos(mm * ph), jnp.exp(mm * lr) * jnp.sin(mm * ph)])
    sc = jnp.transpose(sc.reshape(2, 16, G // 2, 2 * N), (2, 0, 1, 3))
    return kc, pb.astype(BF16), qb.astype(BF16), sc


def _block_transpose(vs, lane_blk):
    vs = list(vs)
    for d in (4, 2, 1):
        take = (lane_blk & d) != 0
        for r in range(len(vs)):
            if r & d:
                continue
            top, bot = vs[r], vs[r + d]
            vs[r] = jnp.where(take, pltpu.roll(bot, d * SSM_GROUP, axis=1), top)
            vs[r + d] = jnp.where(take, bot, pltpu.roll(top, LANES - d * SSM_GROUP, axis=1))
    return vs


def _toeplitz_rows(kc, j, lane):
    lo, hi = kc[:, :LANES], kc[:, LANES:]
    r = (j % SUBLANES) * SSM_GROUP
    if r:
        lo_s, hi_s = pltpu.roll(lo, r, axis=1), pltpu.roll(hi, r, axis=1)
        first = jnp.where(lane < r, 0.0, lo_s)
        second = jnp.where(lane < r, lo_s, hi_s)
    else:
        first, second = lo, hi
    if j < SUBLANES:
        return jnp.concatenate([first, second], axis=1)
    return jnp.concatenate([jnp.zeros_like(first), first], axis=1)


def _s5_core_kernel(hn_ref, kc_ref, p_ref, q_ref, sc_ref, y_ref, t_scr, u_scr, v_scr, x_scr, yg_scr, carry_scr,
                    *, rows, tiles_per_seq):
    L = S5_CHUNK
    n_pair = p_ref.shape[0]
    kk = t_scr.shape[1]
    rb = pl.program_id(1)
    lane_blk = lax.broadcasted_iota(jnp.int32, (rows, LANES), 1) // SSM_GROUP

    @pl.when(rb == 0)
    def _():
        carry_scr[...] = jnp.zeros_like(carry_scr)
        lane = lax.broadcasted_iota(jnp.int32, (SSM_GROUP, LANES), 1)
        for g in range(2 * n_pair):
            kc = kc_ref[g]
            for j in range(L):
                t_scr[g, j * SSM_GROUP:(j + 1) * SSM_GROUP, :] = _toeplitz_rows(kc, j, lane).astype(BF16)

    for jh in range(L // SUBLANES):
        vs = [hn_ref[pl.ds(jh * SUBLANES + r, rows, stride=L), :].astype(BF16) for r in range(SUBLANES)]
        vs = _block_transpose(vs, lane_blk)
        for gl in range(2 * n_pair):
            col = ((gl % 2) * (L // SUBLANES) + jh) * LANES
            u_scr[gl // 2, :, col:col + LANES] = vs[gl]

    for gp in range(n_pair):
        v_scr[gp] = _dot(u_scr[gp], p_ref[gp])

    half = LANES
    row = lax.broadcasted_iota(jnp.int32, (SUBLANES, half), 0)
    tiles = rows // SUBLANES

    def tile_body(t, carry):
        keep = ((rb * tiles + t) % tiles_per_seq) != 0
        r0 = pl.multiple_of(t * SUBLANES, SUBLANES)
        out = []
        for gp in range(n_pair):
            cr = jnp.where(keep, carry[2 * gp], 0.0)
            ci = jnp.where(keep, carry[2 * gp + 1], 0.0)
            r = v_scr[gp, pl.ds(r0, SUBLANES), 0:half]
            i = v_scr[gp, pl.ds(r0, SUBLANES), half:2 * half]
            for s_idx, s in enumerate((1, 2, 4)):
                ar = sc_ref[gp, 0, 8 + s_idx:9 + s_idx, :]
                ai = sc_ref[gp, 1, 8 + s_idx:9 + s_idx, :]
                rs = jnp.where(row >= s, pltpu.roll(r, s, axis=0), 0.0)
                is_ = jnp.where(row >= s, pltpu.roll(i, s, axis=0), 0.0)
                r, i = r + (ar * rs - ai * is_), i + (ar * is_ + ai * rs)
            pw_r = sc_ref[gp, 0, 0:SUBLANES, :]
            pw_i = sc_ref[gp, 1, 0:SUBLANES, :]
            inc_r = r + (pw_r * cr - pw_i * ci)
            inc_i = i + (pw_r * ci + pw_i * cr)
            x_scr[gp, pl.ds(r0, SUBLANES), 0:half] = jnp.where(row == 0, cr, pltpu.roll(inc_r, 1, axis=0))
            x_scr[gp, pl.ds(r0, SUBLANES), half:2 * half] = jnp.where(row == 0, ci, pltpu.roll(inc_i, 1, axis=0))
            last_r = jnp.broadcast_to(r[SUBLANES - 1:SUBLANES, :], (SUBLANES, half))
            last_i = jnp.broadcast_to(i[SUBLANES - 1:SUBLANES, :], (SUBLANES, half))
            a8r = sc_ref[gp, 0, 11:12, :]
            a8i = sc_ref[gp, 1, 11:12, :]
            out += [last_r + (a8r * cr - a8i * ci), last_i + (a8r * ci + a8i * cr)]
        return tuple(out)

    init = tuple(carry_scr[k] for k in range(2 * n_pair))
    fin = lax.fori_loop(0, tiles, tile_body, init, unroll=2)
    for k in range(2 * n_pair):
        carry_scr[k] = fin[k]

    for gp in range(n_pair):
        carry_y = _dot(x_scr[gp].astype(BF16), q_ref[gp])
        for g2 in range(2):
            cols = slice(g2 * kk, (g2 + 1) * kk)
            yg_scr[2 * gp + g2] = _dot(u_scr[gp, :, cols], t_scr[2 * gp + g2]) + carry_y[:, cols]

    for ih in range(L // SUBLANES):
        vs = [yg_scr[gl, :, ih * LANES:(ih + 1) * LANES] for gl in range(2 * n_pair)]
        vs = _block_transpose(vs, lane_blk)
        for r in range(SUBLANES):
            y_ref[pl.ds(ih * SUBLANES + r, rows, stride=L), :] = vs[r]


def _s5_core(hn, kc, pb, qb, sc, *, chunks_per_seq):
    tokens, d = hn.shape
    L = S5_CHUNK
    m = tokens // L
    rows = min(S5_ROWS, m)
    kk = L * SSM_GROUP
    n_pair = LANES // (2 * SSM_GROUP)
    kern = functools.partial(_s5_core_kernel, rows=rows, tiles_per_seq=chunks_per_seq // SUBLANES)
    per_tile = lambda *shape: pl.BlockSpec(shape, lambda lt, rb: (lt,) + (0,) * (len(shape) - 1))
    return pl.pallas_call(
        kern,
        grid=(d // LANES, m // rows),
        in_specs=[
            pl.BlockSpec((rows * L, LANES), lambda lt, rb: (rb, lt)),
            per_tile(2 * n_pair, SSM_GROUP, kk),
            per_tile(n_pair, 2 * kk, kk),
            per_tile(n_pair, kk, 2 * kk),
            per_tile(n_pair, 2, 16, LANES),
        ],
        out_specs=pl.BlockSpec((rows * L, LANES), lambda lt, rb: (rb, lt)),
        out_shape=jax.ShapeDtypeStruct((tokens, d), F32),
        scratch_shapes=[
            pltpu.VMEM((2 * n_pair, kk, kk), BF16),
            pltpu.VMEM((n_pair, rows, 2 * kk), BF16),
            pltpu.VMEM((n_pair, rows, 2 * LANES), F32),
            pltpu.VMEM((n_pair, rows, 2 * LANES), F32),
            pltpu.VMEM((2 * n_pair, rows, kk), F32),
            pltpu.VMEM((2 * n_pair, SUBLANES, LANES), F32),
        ],
        compiler_params=pltpu.CompilerParams(
            dimension_semantics=("arbitrary", "arbitrary"), vmem_limit_bytes=VMEM_LIMIT_BYTES),
        name="s5_core",
    )(hn, kc, pb, qb, sc)


def _prenorm_kernel(x_ref, g_ref, o_ref):
    o_ref[...] = _rms(x_ref[...], g_ref[...]).astype(o_ref.dtype)


def _prenorm(x, g):
    t, d = x.shape
    tb = min(t, 1024)
    return pl.pallas_call(
        _prenorm_kernel,
        grid=(t // tb,),
        in_specs=[pl.BlockSpec((tb, d), lambda i: (i, 0)), pl.BlockSpec((1, d), lambda i: (0, 0))],
        out_specs=pl.BlockSpec((tb, d), lambda i: (i, 0)),
        out_shape=jax.ShapeDtypeStruct((t, d), F32),
        compiler_params=pltpu.CompilerParams(dimension_semantics=("arbitrary",)),
        name="prenorm",
    )(x, g)


def _rope(x, cos_t, sin_t, l64):
    half = ROT_DIM // 2
    lo = pltpu.roll(x, half, axis=1)
    hi = pltpu.roll(x, LANES - half, axis=1)
    return x * cos_t + jnp.where(l64 < half, -hi, lo) * sin_t


def _rope_tables(inv_ref, row0, nrows, seq):
    pos = (row0 + lax.broadcasted_iota(jnp.int32, (nrows, LANES), 0)) % seq
    ang = pos.astype(F32) * inv_ref[...]
    l64 = lax.broadcasted_iota(jnp.int32, (nrows, LANES), 1) % HEAD_DIM
    return jnp.cos(ang), jnp.sin(ang), l64


def _attention_consts():
    nq = ATTN_BLOCK
    ii = lax.broadcasted_iota(jnp.int32, (nq, 2 * nq), 0)
    jj = lax.broadcasted_iota(jnp.int32, (nq, 2 * nq), 1) % nq
    cur = jj <= ii
    lane = lax.broadcasted_iota(jnp.int32, (nq, LANES), 1)
    first = lane < HEAD_DIM
    return dict(ii=ii, jj=jj, cur=cur, first=first, zero=jnp.zeros((nq, LANES), BF16),
                cur_bf=jnp.where(cur, 1.0, 0.0).astype(BF16), prev_bf=jnp.where(cur, 0.0, 1.0).astype(BF16),
                ones_a=jnp.where(first, 1.0, 0.0).astype(BF16), ones_b=jnp.where(first, 0.0, 1.0).astype(BF16))


def _kv_blocks(k, v, c):
    first, zero = c["first"], c["zero"]
    k_blk = jnp.concatenate([jnp.where(first, k, zero), jnp.where(first, zero, k)], axis=0)
    v_blk = jnp.concatenate([
        jnp.concatenate([jnp.where(first, v, zero), c["ones_a"]], axis=1),
        jnp.concatenate([jnp.where(first, zero, v), c["ones_b"]], axis=1)], axis=0)
    return k_blk, v_blk


def _attention(q_ref, kc_ref, kp_ref, vc_ref, vp_ref, sink_ref, o_scr, c, *, row0, nrows, tb, seq):
    nq = ATTN_BLOCK
    nt = (((1,), (1,)), ((), ()))
    blocks = {}

    def kv(kh, b):
        if (kh, b) not in blocks:
            if b < 0:
                blocks[kh, b] = _kv_blocks(kp_ref[kh], vp_ref[kh], c)
            else:
                rows = slice(b * nq, (b + 1) * nq)
                blocks[kh, b] = _kv_blocks(kc_ref[kh, rows, :], vc_ref[kh, rows, :], c)
        return blocks[kh, b]

    for b in range(row0 // nq, (row0 + nrows) // nq):
        rows = slice(b * nq, (b + 1) * nq)
        has_prev = ((pl.program_id(0) * (tb // nq) + b) % (seq // nq)) != 0
        valid = c["jj"] <= c["ii"] + jnp.where(has_prev, nq, 0)
        for kh in range(N_KV_HEADS):
            k_cur, v_cur = kv(kh, b)
            k_prev, v_prev = kv(kh, b - 1)
            for pair in range(GQA_GROUP // 2):
                hp = kh * (GQA_GROUP // 2) + pair
                qp = q_ref[hp, rows, :]
                s_cur = lax.dot_general(qp, k_cur, nt, preferred_element_type=F32)
                s_prev = lax.dot_general(qp, k_prev, nt, preferred_element_type=F32)
                s = jnp.where(valid, jnp.where(c["cur"], s_cur, s_prev), NEG_INF)
                es, sinks = [], []
                for hh in range(2):
                    sh = s[:, hh * nq:(hh + 1) * nq]
                    sink = sink_ref[2 * hp + hh]
                    mx = jnp.maximum(jnp.max(sh, axis=-1, keepdims=True), sink)
                    es.append(jnp.exp(sh - mx))
                    sinks.append(jnp.exp(sink - mx))
                e = jnp.concatenate(es, axis=1).astype(BF16)
                o = _dot(e * c["cur_bf"], v_cur) + _dot(e * c["prev_bf"], v_prev)
                den = o[:, LANES:] + jnp.where(c["first"], sinks[0], sinks[1])
                o_scr[rows, hp * LANES:(hp + 1) * LANES] = (o[:, :LANES] / den).astype(o_scr.dtype)


def _layer_kernel(*refs, mixer, epilogue, tb, seq):
    it = iter(refs)
    h_ref = next(it)
    p_ref = next(it)
    if mixer == "s5":
        y_ref, gmix_ref, d_ref, wglu_ref = next(it), next(it), next(it), next(it)
    else:
        q_ref, kc_ref, kp_ref, vc_ref, vp_ref, sink_ref, wo_ref = (next(it) for _ in range(7))
    gmlp_ref, wup_ref, wdown_ref, gple_ref, wgate_ref, wproj_ref = (next(it) for _ in range(6))
    if epilogue == "norm":
        gnext_ref = next(it)
    elif epilogue == "q":
        gnext_ref, wq_ref, inv_ref = next(it), next(it), next(it)
    elif epilogue == "kvq":
        gnext_ref, wq_ref, inv_ref, gkv_ref, wk_ref, wv_ref = (next(it) for _ in range(6))
    elif epilogue == "final":
        gnext_ref = next(it)
    ho_ref = None if epilogue == "final" else next(it)
    if epilogue == "norm":
        hn_ref = next(it)
    elif epilogue == "q":
        qo_ref = next(it)
    elif epilogue == "kvq":
        qo_ref, ko_ref, vo_ref = next(it), next(it), next(it)
    elif epilogue == "final":
        out_ref = next(it)
    if mixer == "attn":
        o_scr = next(it)

    d_model = h_ref.shape[-1]
    d_ff = wup_ref.shape[1]
    sub = min(SUB_BLOCK, tb)
    consts = _attention_consts() if mixer == "attn" else None

    def chain(r0):
        rows = slice(r0, r0 + sub)
        h = h_ref[rows, :]

        if mixer == "s5":
            u = _rms(h, gmix_ref[...])
            ge = _gelu_tanh(y_ref[rows, :] + d_ref[...] * u).astype(BF16)
            ab = _dot(ge, wglu_ref[...])
            h = h + ab[:, :d_model] * jax.nn.sigmoid(ab[:, d_model:])
        else:
            _attention(q_ref, kc_ref, kp_ref, vc_ref, vp_ref, sink_ref, o_scr, consts,
                       row0=r0, nrows=sub, tb=tb, seq=seq)
            mix = _dot(o_scr[rows, :], wo_ref[...])
            h = h + mix

        hm = _rms(h, gmlp_ref[...]).astype(BF16)
        acc = h
        for c0 in range(0, d_ff, FF_CHUNK):
            up = _dot(hm, wup_ref[:, c0:c0 + FF_CHUNK])
            act = jnp.square(jnp.maximum(up, 0.0)).astype(BF16)
            acc = acc + _dot(act, wdown_ref[c0:c0 + FF_CHUNK, :])
        h = acc

        hg = _rms(h, gple_ref[...]).astype(BF16)
        pe = p_ref[rows, :].astype(BF16)
        gate = _dot(hg, wgate_ref[...])
        proj = _dot(pe, wproj_ref[...])
        h = h + jax.nn.sigmoid(gate) * proj

        if epilogue == "final":
            out_ref[rows, :] = _rms(h, gnext_ref[...])
            return
        ho_ref[rows, :] = h
        if epilogue == "norm":
            hn_ref[rows, :] = _rms(h, gnext_ref[...]).astype(hn_ref.dtype)
            return
        cos_t, sin_t, l64 = _rope_tables(inv_ref, pl.program_id(0) * tb + r0, sub, seq)
        hq = _rms(h, gnext_ref[...]).astype(BF16)
        hk = _rms(h, gkv_ref[...]).astype(BF16) if epilogue == "kvq" else None
        q = _dot(hq, wq_ref[...])
        if epilogue == "kvq":
            k = _dot(hk, wk_ref[...])
            v = _dot(hk, wv_ref[...])
        scale = HEAD_DIM ** -0.5
        for hp in range(d_model // LANES):
            tile = _rope(q[:, hp * LANES:(hp + 1) * LANES], cos_t, sin_t, l64)
            qo_ref[hp, rows, :] = (tile * scale).astype(qo_ref.dtype)
        if epilogue == "kvq":
            for kh in range(N_KV_HEADS):
                tile = _rope(k[:, kh * LANES:(kh + 1) * LANES], cos_t, sin_t, l64)
                ko_ref[kh, rows, :] = tile.astype(ko_ref.dtype)
                vo_ref[kh, rows, :] = v[:, kh * LANES:(kh + 1) * LANES].astype(vo_ref.dtype)

    for r0 in range(0, tb, sub):
        chain(r0)


def _weight(w):
    if not isinstance(w, tuple):
        return w, _resident(w.shape)
    arr, idx = w
    nd = arr.ndim - 1
    return arr, pl.BlockSpec((None,) + arr.shape[1:], lambda i: (idx,) + (0,) * nd, pipeline_mode=pl.Buffered(1))


def _layer_call(h, p, mixer_args, tail_args, epi_args, *, mixer, epilogue, seq):
    t, d = h.shape
    tb = min(TOKEN_BLOCK, seq)
    nsteps = t // tb
    row = lambda w: pl.BlockSpec((tb, w), lambda i: (i, 0))
    vec = lambda: pl.BlockSpec((1, d), lambda i: (0, 0))
    heads = lambda n: pl.BlockSpec((n, tb, LANES), lambda i: (0, i, 0))

    args, specs = [], []

    def add(arr, spec):
        args.append(arr)
        specs.append(spec)

    def add_weight(w):
        add(*_weight(w))

    p_all, layer = p
    add(h, row(d))
    add(p_all, pl.BlockSpec((None, tb, p_all.shape[-1]), lambda i: (layer, i, 0)))
    if mixer == "s5":
        y, gmix, dskip, wglu = mixer_args
        add(y, row(d))
        add(gmix, vec())
        add(dskip, vec())
        add_weight(wglu)
    else:
        q, kd, vd, sinks, wo = mixer_args
        blocks_per_tb = tb // ATTN_BLOCK
        prev = lambda: pl.BlockSpec((N_KV_HEADS, ATTN_BLOCK, LANES),
                                    lambda i: (0, jnp.maximum(i * blocks_per_tb - 1, 0), 0))
        add(q, heads(d // LANES))
        add(kd, heads(N_KV_HEADS))
        add(kd, prev())
        add(vd, heads(N_KV_HEADS))
        add(vd, prev())
        add(sinks, pl.BlockSpec(memory_space=pltpu.SMEM))
        add_weight(wo)
    gmlp, wup, wdown, gple, wgate, wproj = tail_args
    add(gmlp, vec())
    add_weight(wup)
    add_weight(wdown)
    add(gple, vec())
    add_weight(wgate)
    add_weight(wproj)

    out_shapes, out_specs = [], []
    if epilogue != "final":
        out_shapes.append(jax.ShapeDtypeStruct((t, d), F32))
        out_specs.append(row(d))
    add(epi_args[0], vec())
    if epilogue in ("norm", "final"):
        out_shapes.append(jax.ShapeDtypeStruct((t, d), F32))
        out_specs.append(row(d))
    else:
        add_weight(epi_args[1])
        add(epi_args[2], pl.BlockSpec((1, LANES), lambda i: (0, 0)))
        out_shapes.append(jax.ShapeDtypeStruct((d // LANES, t, LANES), BF16))
        out_specs.append(heads(d // LANES))
        if epilogue == "kvq":
            gkv, wk, wv = epi_args[3:]
            add(gkv, vec())
            add_weight(wk)
            add_weight(wv)
            out_shapes += [jax.ShapeDtypeStruct((N_KV_HEADS, t, LANES), BF16)] * 2
            out_specs += [heads(N_KV_HEADS), heads(N_KV_HEADS)]

    scratch = [pltpu.VMEM((tb, d), BF16)] if mixer == "attn" else []
    kern = functools.partial(_layer_kernel, mixer=mixer, epilogue=epilogue, tb=tb, seq=seq)
    return pl.pallas_call(
        kern,
        grid=(nsteps,),
        in_specs=specs,
        out_specs=out_specs,
        out_shape=out_shapes,
        scratch_shapes=scratch,
        compiler_params=pltpu.CompilerParams(
            dimension_semantics=("arbitrary",), vmem_limit_bytes=VMEM_LIMIT_BYTES),
        name=f"layer_{mixer}_{epilogue}",
    )(*args)


def kernel(x, p, norm_mix, ssm_lambda_re, ssm_lambda_im, ssm_log_dt, ssm_b_re, ssm_b_im, ssm_c_re, ssm_c_im, ssm_d, ssm_w_glu, kv_norm, w_k, w_v, w_q, attn_sinks, w_o, norm_mlp, w_up, w_down, norm_ple, w_ple_gate, w_ple_proj, norm_final):
    bsz, seq, d = x.shape
    depth = p.shape[0]
    n_a = ssm_lambda_re.shape[0]
    t = bsz * seq
    L = S5_CHUNK
    assert seq % (L * SUBLANES) == 0 and seq % ATTN_BLOCK == 0 and d % LANES == 0

    vec = lambda g: g.reshape(1, d)
    inv = ROPE_THETA ** (-jnp.arange(0, ROT_DIM, 2, dtype=F32) / ROT_DIM)
    inv64 = jnp.concatenate([inv, inv, jnp.zeros((HEAD_DIM - ROT_DIM,), F32)])
    inv_lanes = jnp.tile(inv64, LANES // HEAD_DIM).reshape(1, LANES)
    dup = lambda w: jnp.repeat(w.reshape(d, N_KV_HEADS, 1, HEAD_DIM), 2, axis=2).reshape(d, 2 * N_KV_HEADS * HEAD_DIM)

    p_all = p.reshape(depth, t, p.shape[-1])
    glu_bf, up_bf, down_bf = ssm_w_glu.astype(BF16), w_up.astype(BF16), w_down.astype(BF16)
    gate_bf, proj_bf = w_ple_gate.astype(BF16), w_ple_proj.astype(BF16)
    q_bf, o_bf = w_q.astype(BF16), w_o.astype(BF16)

    h = x.reshape(t, d)
    hn = _prenorm(h, vec(norm_mix[0]))
    q = kd = vd = None
    out = None
    for i in range(depth):
        tail = (vec(norm_mlp[i]), (up_bf, i), (down_bf, i), vec(norm_ple[i]), (gate_bf, i), (proj_bf, i))
        if i == depth - 1:
            epilogue, epi = "final", (vec(norm_final),)
        elif i + 1 < n_a:
            epilogue, epi = "norm", (vec(norm_mix[i + 1]),)
        elif i + 1 == n_a:
            epilogue = "kvq"
            epi = (vec(norm_mix[i + 1]), (q_bf, 0), inv_lanes,
                   vec(kv_norm), dup(w_k).astype(BF16), dup(w_v).astype(BF16))
        else:
            epilogue, epi = "q", (vec(norm_mix[i + 1]), (q_bf, i + 1 - n_a), inv_lanes)

        if i < n_a:
            kc, pb, qb, sc = _s5_tables(ssm_lambda_re[i], ssm_lambda_im[i], ssm_log_dt[i],
                                        ssm_b_re[i], ssm_b_im[i], ssm_c_re[i], ssm_c_im[i])
            y = _s5_core(hn, kc, pb, qb, sc, chunks_per_seq=seq // L)
            mixer, mix = "s5", (y, vec(norm_mix[i]), vec(ssm_d[i]), (glu_bf, i))
        else:
            j = i - n_a
            mixer, mix = "attn", (q, kd, vd, attn_sinks[j], (o_bf, j))

        res = _layer_call(h, (p_all, i), mix, tail, epi, mixer=mixer, epilogue=epilogue, seq=seq)
        if epilogue == "final":
            out = res[0]
        elif epilogue == "norm":
            h, hn = res
        elif epilogue == "q":
            h, q = res
        else:
            h, q, kd, vd = res
    return out.reshape(bsz, seq, d)
```

```python
import functools
import math

import jax
import jax.numpy as jnp
from jax import lax
from jax.experimental import pallas as pl
from jax.experimental.pallas import tpu as pltpu

SSM_GROUP = 16
SSM_STATE = 64
HEAD_DIM = 64
N_KV_HEADS = 4
GQA_GROUP = 4
ATTN_BLOCK = 128
WINDOW = 128
ROPE_THETA = 500000.0
ROT_DIM = 16
RMS_EPS = 1e-6
NEG_INF = -1e30

LANES = 128
SUBLANES = 8
VMEM_LIMIT_BYTES = 56 * 1024 * 1024

S5_CHUNK = 16
S5_ROWS = 512
TOKEN_BLOCK = 512
SUB_BLOCK = 256
FF_CHUNK = 1024
GLU_TILE = 256

BF16 = jnp.bfloat16
F32 = jnp.float32


def _dot(a, b):
    return jnp.dot(a, b, preferred_element_type=F32)


def _inv_rms(x):
    return lax.rsqrt(jnp.mean(x * x, axis=-1, keepdims=True) + RMS_EPS)


def _rms(x, g):
    return x * _inv_rms(x) * g


def _gelu_tanh(x):
    c = math.sqrt(2.0 / math.pi)
    return 0.5 * x * (1.0 + jnp.tanh(c * (x + 0.044715 * (x * x * x))))


def _resident(shape):
    nd = len(shape)
    return pl.BlockSpec(shape, lambda i: (0,) * nd, pipeline_mode=pl.Buffered(1))


def _s5_tables(lam_re, lam_im, log_dt, b_re, b_im, c_re, c_im):
    hp = lax.Precision.HIGHEST
    L = S5_CHUNK
    G, N = lam_re.shape
    H = b_re.shape[-1]
    dt = jnp.exp(log_dt)[:, None]
    lr, ph = lam_re * dt, lam_im * dt
    d = jnp.arange(L + 1, dtype=F32)[:, None, None]
    pw_r = jnp.exp(d * lr) * jnp.cos(d * ph)
    pw_i = jnp.exp(d * lr) * jnp.sin(d * ph)
    a_r, a_i = pw_r[1], pw_i[1]
    den = lam_re * lam_re + lam_im * lam_im
    nr = a_r - 1.0
    coef_r = (nr * lam_re + a_i * lam_im) / den
    coef_i = (a_i * lam_re - nr * lam_im) / den
    bb_r = coef_r[..., None] * b_re - coef_i[..., None] * b_im
    bb_i = coef_r[..., None] * b_im + coef_i[..., None] * b_re
    ab_r = pw_r[:L, :, :, None] * bb_r - pw_i[:L, :, :, None] * bb_i
    ab_i = pw_r[:L, :, :, None] * bb_i + pw_i[:L, :, :, None] * bb_r
    kc = (jnp.einsum('dgnh,gkn->ghdk', ab_r, c_re, precision=hp)
          - jnp.einsum('dgnh,gkn->ghdk', ab_i, c_im, precision=hp)).reshape(G, H, L * H)

    p_r = jnp.transpose(ab_r[::-1], (1, 0, 3, 2)).reshape(G // 2, 2, L * H, N)
    p_i = jnp.transpose(ab_i[::-1], (1, 0, 3, 2)).reshape(G // 2, 2, L * H, N)
    zp = jnp.zeros_like(p_r[:, 0])
    pb = jnp.concatenate([jnp.concatenate([p_r[:, 0], zp, p_i[:, 0], zp], axis=-1),
                          jnp.concatenate([zp, p_r[:, 1], zp, p_i[:, 1]], axis=-1)], axis=1)

    ca_r = c_re[None] * pw_r[1:, :, None, :] - c_im[None] * pw_i[1:, :, None, :]
    ca_i = c_re[None] * pw_i[1:, :, None, :] + c_im[None] * pw_r[1:, :, None, :]
    q_r = jnp.transpose(ca_r, (1, 3, 0, 2)).reshape(G // 2, 2, N, L * H)
    q_i = jnp.transpose(-ca_i, (1, 3, 0, 2)).reshape(G // 2, 2, N, L * H)
    zq = jnp.zeros_like(q_r[:, 0])
    qb = jnp.concatenate([jnp.concatenate([q_r[:, 0], zq], axis=-1), jnp.concatenate([zq, q_r[:, 1]], axis=-1),
                          jnp.concatenate([q_i[:, 0], zq], axis=-1), jnp.concatenate([zq, q_i[:, 1]], axis=-1)],
                         axis=1)

    mm = jnp.array([1, 2, 3, 4, 5, 6, 7, 8, 1, 2, 4, 8, 0, 0, 0, 0], F32)[:, None, None] * L
    sc = jnp.stack([jnp.exp(mm * lr) * jnp.cos(mm * ph), jnp.exp(mm * lr) * jnp.sin(mm * ph)])
    sc = jnp.transpose(sc.reshape(2, 16, G // 2, 2 * N), (2, 0, 1, 3))
    return kc, pb.astype(BF16), qb.astype(BF16), sc


def _block_transpose(vs, lane_blk):
    vs = list(vs)
    for d in (4, 2, 1):
        take = (lane_blk & d) != 0
        for r in range(len(vs)):
            if r & d:
                continue
            top, bot = vs[r], vs[r + d]
            vs[r] = jnp.where(take, pltpu.roll(bot, d * SSM_GROUP, axis=1), top)
            vs[r + d] = jnp.where(take, bot, pltpu.roll(top, LANES - d * SSM_GROUP, axis=1))
    return vs


def _toeplitz_rows(kc, j, lane):
    lo, hi = kc[:, :LANES], kc[:, LANES:]
    r = (j % SUBLANES) * SSM_GROUP
    if r:
        lo_s, hi_s = pltpu.roll(lo, r, axis=1), pltpu.roll(hi, r, axis=1)
        first = jnp.where(lane < r, 0.0, lo_s)
        second = jnp.where(lane < r, lo_s, hi_s)
    else:
        first, second = lo, hi
    if j < SUBLANES:
        return jnp.concatenate([first, second], axis=1)
    return jnp.concatenate([jnp.zeros_like(first), first], axis=1)


def _s5_core_kernel(hn_ref, kc_ref, p_ref, q_ref, sc_ref, y_ref, t_scr, u_scr, v_scr, x_scr, yg_scr, carry_scr,
                    *, rows, tiles_per_seq):
    L = S5_CHUNK
    n_pair = p_ref.shape[0]
    kk = t_scr.shape[1]
    rb = pl.program_id(1)
    lane_blk = lax.broadcasted_iota(jnp.int32, (rows, LANES), 1) // SSM_GROUP

    @pl.when(rb == 0)
    def _():
        carry_scr[...] = jnp.zeros_like(carry_scr)
        lane = lax.broadcasted_iota(jnp.int32, (SSM_GROUP, LANES), 1)
        for g in range(2 * n_pair):
            kc = kc_ref[g]
            for j in range(L):
                t_scr[g, j * SSM_GROUP:(j + 1) * SSM_GROUP, :] = _toeplitz_rows(kc, j, lane).astype(BF16)

    for jh in range(L // SUBLANES):
        vs = [hn_ref[pl.ds(jh * SUBLANES + r, rows, stride=L), :].astype(BF16) for r in range(SUBLANES)]
        vs = _block_transpose(vs, lane_blk)
        for gl in range(2 * n_pair):
            col = ((gl % 2) * (L // SUBLANES) + jh) * LANES
            u_scr[gl // 2, :, col:col + LANES] = vs[gl]

    for gp in range(n_pair):
        v_scr[gp] = _dot(u_scr[gp], p_ref[gp])

    half = LANES
    row = lax.broadcasted_iota(jnp.int32, (SUBLANES, half), 0)
    tiles = rows // SUBLANES

    def tile_body(t, carry):
        keep = ((rb * tiles + t) % tiles_per_seq) != 0
        r0 = pl.multiple_of(t * SUBLANES, SUBLANES)
        out = []
        for gp in range(n_pair):
            cr = jnp.where(keep, carry[2 * gp], 0.0)
            ci = jnp.where(keep, carry[2 * gp + 1], 0.0)
            r = v_scr[gp, pl.ds(r0, SUBLANES), 0:half]
            i = v_scr[gp, pl.ds(r0, SUBLANES), half:2 * half]
            for s_idx, s in enumerate((1, 2, 4)):
                ar = sc_ref[gp, 0, 8 + s_idx:9 + s_idx, :]
                ai = sc_ref[gp, 1, 8 + s_idx:9 + s_idx, :]
                rs = jnp.where(row >= s, pltpu.roll(r, s, axis=0), 0.0)
                is_ = jnp.where(row >= s, pltpu.roll(i, s, axis=0), 0.0)
                r, i = r + (ar * rs - ai * is_), i + (ar * is_ + ai * rs)
            pw_r = sc_ref[gp, 0, 0:SUBLANES, :]
            pw_i = sc_ref[gp, 1, 0:SUBLANES, :]
            inc_r = r + (pw_r * cr - pw_i * ci)
            inc_i = i + (pw_r * ci + pw_i * cr)
            x_scr[gp, pl.ds(r0, SUBLANES), 0:half] = jnp.where(row == 0, cr, pltpu.roll(inc_r, 1, axis=0))
            x_scr[gp, pl.ds(r0, SUBLANES), half:2 * half] = jnp.where(row == 0, ci, pltpu.roll(inc_i, 1, axis=0))
            last_r = jnp.broadcast_to(r[SUBLANES - 1:SUBLANES, :], (SUBLANES, half))
            last_i = jnp.broadcast_to(i[SUBLANES - 1:SUBLANES, :], (SUBLANES, half))
            a8r = sc_ref[gp, 0, 11:12, :]
            a8i = sc_ref[gp, 1, 11:12, :]
            out += [last_r + (a8r * cr - a8i * ci), last_i + (a8r * ci + a8i * cr)]
        return tuple(out)

    init = tuple(carry_scr[k] for k in range(2 * n_pair))
    fin = lax.fori_loop(0, tiles, tile_body, init, unroll=2)
    for k in range(2 * n_pair):
        carry_scr[k] = fin[k]

    for gp in range(n_pair):
        carry_y = _dot(x_scr[gp].astype(BF16), q_ref[gp])
        for g2 in range(2):
            cols = slice(g2 * kk, (g2 + 1) * kk)
            yg_scr[2 * gp + g2] = _dot(u_scr[gp, :, cols], t_scr[2 * gp + g2]) + carry_y[:, cols]

    for ih in range(L // SUBLANES):
        vs = [yg_scr[gl, :, ih * LANES:(ih + 1) * LANES] for gl in range(2 * n_pair)]
        vs = _block_transpose(vs, lane_blk)
        for r in range(SUBLANES):
            y_ref[pl.ds(ih * SUBLANES + r, rows, stride=L), :] = vs[r]


def _s5_core(hn, kc, pb, qb, sc, *, chunks_per_seq):
    tokens, d = hn.shape
    L = S5_CHUNK
    m = tokens // L
    rows = min(S5_ROWS, m)
    kk = L * SSM_GROUP
    n_pair = LANES // (2 * SSM_GROUP)
    kern = functools.partial(_s5_core_kernel, rows=rows, tiles_per_seq=chunks_per_seq // SUBLANES)
    per_tile = lambda *shape: pl.BlockSpec(shape, lambda lt, rb: (lt,) + (0,) * (len(shape) - 1))
    return pl.pallas_call(
        kern,
        grid=(d // LANES, m // rows),
        in_specs=[
            pl.BlockSpec((rows * L, LANES), lambda lt, rb: (rb, lt)),
            per_tile(2 * n_pair, SSM_GROUP, kk),
            per_tile(n_pair, 2 * kk, kk),
            per_tile(n_pair, kk, 2 * kk),
            per_tile(n_pair, 2, 16, LANES),
        ],
        out_specs=pl.BlockSpec((rows * L, LANES), lambda lt, rb: (rb, lt)),
        out_shape=jax.ShapeDtypeStruct((tokens, d), F32),
        scratch_shapes=[
            pltpu.VMEM((2 * n_pair, kk, kk), BF16),
            pltpu.VMEM((n_pair, rows, 2 * kk), BF16),
            pltpu.VMEM((n_pair, rows, 2 * LANES), F32),
            pltpu.VMEM((n_pair, rows, 2 * LANES), F32),
            pltpu.VMEM((2 * n_pair, rows, kk), F32),
            pltpu.VMEM((2 * n_pair, SUBLANES, LANES), F32),
        ],
        compiler_params=pltpu.CompilerParams(
            dimension_semantics=("arbitrary", "arbitrary"), vmem_limit_bytes=VMEM_LIMIT_BYTES),
        name="s5_core",
    )(hn, kc, pb, qb, sc)


def _prenorm_kernel(x_ref, g_ref, o_ref):
    o_ref[...] = _rms(x_ref[...], g_ref[...]).astype(o_ref.dtype)


def _prenorm(x, g):
    t, d = x.shape
    tb = min(t, 1024)
    return pl.pallas_call(
        _prenorm_kernel,
        grid=(t // tb,),
        in_specs=[pl.BlockSpec((tb, d), lambda i: (i, 0)), pl.BlockSpec((1, d), lambda i: (0, 0))],
        out_specs=pl.BlockSpec((tb, d), lambda i: (i, 0)),
        out_shape=jax.ShapeDtypeStruct((t, d), F32),
        compiler_params=pltpu.CompilerParams(dimension_semantics=("arbitrary",)),
        name="prenorm",
    )(x, g)


def _rope(x, cos_t, sin_t, l64):
    half = ROT_DIM // 2
    lo = pltpu.roll(x, half, axis=1)
    hi = pltpu.roll(x, LANES - half, axis=1)
    return x * cos_t + jnp.where(l64 < half, -hi, lo) * sin_t


def _rope_tables(inv_ref, row0, nrows, seq):
    pos = (row0 + lax.broadcasted_iota(jnp.int32, (nrows, LANES), 0)) % seq
    ang = pos.astype(F32) * inv_ref[...]
    l64 = lax.broadcasted_iota(jnp.int32, (nrows, LANES), 1) % HEAD_DIM
    return jnp.cos(ang), jnp.sin(ang), l64


def _attention_consts():
    nq = ATTN_BLOCK
    ii = lax.broadcasted_iota(jnp.int32, (nq, 2 * nq), 0)
    jj = lax.broadcasted_iota(jnp.int32, (nq, 2 * nq), 1) % nq
    cur = jj <= ii
    lane = lax.broadcasted_iota(jnp.int32, (nq, LANES), 1)
    first = lane < HEAD_DIM
    return dict(ii=ii, jj=jj, cur=cur, first=first, zero=jnp.zeros((nq, LANES), BF16),
                cur_bf=jnp.where(cur, 1.0, 0.0).astype(BF16), prev_bf=jnp.where(cur, 0.0, 1.0).astype(BF16),
                ones_a=jnp.where(first, 1.0, 0.0).astype(BF16), ones_b=jnp.where(first, 0.0, 1.0).astype(BF16))


def _kv_blocks(k, v, c):
    first, zero = c["first"], c["zero"]
    k_blk = jnp.concatenate([jnp.where(first, k, zero), jnp.where(first, zero, k)], axis=0)
    v_blk = jnp.concatenate([
        jnp.concatenate([jnp.where(first, v, zero), c["ones_a"]], axis=1),
        jnp.concatenate([jnp.where(first, zero, v), c["ones_b"]], axis=1)], axis=0)
    return k_blk, v_blk


def _attention(q_ref, kc_ref, kp_ref, vc_ref, vp_ref, sink_ref, o_scr, c, *, row0, nrows, tb, seq):
    nq = ATTN_BLOCK
    pairs = GQA_GROUP // 2
    nt = (((1,), (1,)), ((), ()))
    blocks = {}

    def kv(kh, b):
        if (kh, b) not in blocks:
            if b < 0:
                blocks[kh, b] = _kv_blocks(kp_ref[kh], vp_ref[kh], c)
            else:
                rows = slice(b * nq, (b + 1) * nq)
                blocks[kh, b] = _kv_blocks(kc_ref[kh, rows, :], vc_ref[kh, rows, :], c)
        return blocks[kh, b]

    for b in range(row0 // nq, (row0 + nrows) // nq):
        rows = slice(b * nq, (b + 1) * nq)
        has_prev = ((pl.program_id(0) * (tb // nq) + b) % (seq // nq)) != 0
        valid = c["jj"] <= c["ii"] + jnp.where(has_prev, nq, 0)
        for kh in range(N_KV_HEADS):
            k_cur, v_cur = kv(kh, b)
            k_prev, v_prev = kv(kh, b - 1)
            for pair in range(pairs):
                hp = kh * pairs + pair
                qp = q_ref[hp, rows, :]
                s_cur = lax.dot_general(qp, k_cur, nt, preferred_element_type=F32)
                s_prev = lax.dot_general(qp, k_prev, nt, preferred_element_type=F32)
                s = jnp.where(valid, jnp.where(c["cur"], s_cur, s_prev), NEG_INF)
                es, sinks = [], []
                for hh in range(2):
                    sh = s[:, hh * nq:(hh + 1) * nq]
                    sink = sink_ref[2 * hp + hh]
                    mx = jnp.maximum(jnp.max(sh, axis=-1, keepdims=True), sink)
                    es.append(jnp.exp(sh - mx))
                    sinks.append(jnp.exp(sink - mx))
                e = jnp.concatenate(es, axis=1).astype(BF16)
                o = _dot(e * c["cur_bf"], v_cur) + _dot(e * c["prev_bf"], v_prev)
                den = o[:, LANES:] + jnp.where(c["first"], sinks[0], sinks[1])
                o_scr[rows, hp * LANES:(hp + 1) * LANES] = (o[:, :LANES] / den).astype(o_scr.dtype)


def _layer_kernel(*refs, mixer, epilogue, tb, seq):
    it = iter(refs)
    h_ref = next(it)
    p_ref = next(it)
    if mixer == "s5":
        y_ref, gmix_ref, d_ref, wglu_ref = next(it), next(it), next(it), next(it)
    else:
        q_ref, kc_ref, kp_ref, vc_ref, vp_ref, sink_ref, wo_ref = (next(it) for _ in range(7))
    wup_ref, wdown_ref, wgate_ref, wproj_ref = (next(it) for _ in range(4))
    if epilogue in ("norm", "final"):
        gnext_ref = next(it)
    else:
        wq_ref, inv_ref = next(it), next(it)
        if epilogue == "kvq":
            wk_ref, wv_ref = next(it), next(it)
    ho_ref = None if epilogue == "final" else next(it)
    if epilogue == "norm":
        hn_ref = next(it)
    elif epilogue == "q":
        qo_ref = next(it)
    elif epilogue == "kvq":
        qo_ref, ko_ref, vo_ref = next(it), next(it), next(it)
    elif epilogue == "final":
        out_ref = next(it)
    if mixer == "attn":
        o_scr = next(it)

    d_model = h_ref.shape[-1]
    d_ff = wup_ref.shape[1]
    sub = min(SUB_BLOCK, tb)
    consts = _attention_consts() if mixer == "attn" else None

    def chain(r0):
        rows = slice(r0, r0 + sub)
        h = h_ref[rows, :]

        if mixer == "s5":
            u = _rms(h, gmix_ref[...])
            ge = _gelu_tanh(y_ref[rows, :] + d_ref[...] * u).astype(BF16)
            ab = _dot(ge, wglu_ref[...])
            mix = [ab[:, (2 * k) * GLU_TILE:(2 * k + 1) * GLU_TILE]
                   * jax.nn.sigmoid(ab[:, (2 * k + 1) * GLU_TILE:(2 * k + 2) * GLU_TILE])
                   for k in range(d_model // GLU_TILE)]
            h = h + jnp.concatenate(mix, axis=1)
        else:
            _attention(q_ref, kc_ref, kp_ref, vc_ref, vp_ref, sink_ref, o_scr, consts,
                       row0=r0, nrows=sub, tb=tb, seq=seq)
            h = h + _dot(o_scr[rows, :], wo_ref[...])

        s = _inv_rms(h)
        hb = h.astype(BF16)
        acc = None
        for c0 in range(0, d_ff, FF_CHUNK):
            up = _dot(hb, wup_ref[:, c0:c0 + FF_CHUNK])
            act = jnp.square(jnp.maximum(up, 0.0)).astype(BF16)
            part = _dot(act, wdown_ref[c0:c0 + FF_CHUNK, :])
            acc = part if acc is None else acc + part
        h = h + (s * s) * acc

        gate = _inv_rms(h) * _dot(h.astype(BF16), wgate_ref[...])
        proj = _dot(p_ref[rows, :].astype(BF16), wproj_ref[...])
        h = h + jax.nn.sigmoid(gate) * proj

        if epilogue == "final":
            out_ref[rows, :] = _rms(h, gnext_ref[...])
            return
        ho_ref[rows, :] = h
        if epilogue == "norm":
            hn_ref[rows, :] = _rms(h, gnext_ref[...]).astype(hn_ref.dtype)
            return
        cos_t, sin_t, l64 = _rope_tables(inv_ref, pl.program_id(0) * tb + r0, sub, seq)
        hb = h.astype(BF16)
        s = _inv_rms(h)
        q = _dot(hb, wq_ref[...])
        if epilogue == "kvq":
            k = _dot(hb, wk_ref[...])
            v = _dot(hb, wv_ref[...])
        qs = s * HEAD_DIM ** -0.5
        for hp in range(d_model // LANES):
            tile = _rope(q[:, hp * LANES:(hp + 1) * LANES] * qs, cos_t, sin_t, l64)
            qo_ref[hp, rows, :] = tile.astype(qo_ref.dtype)
        if epilogue == "kvq":
            for kh in range(N_KV_HEADS):
                tile = _rope(k[:, kh * LANES:(kh + 1) * LANES] * s, cos_t, sin_t, l64)
                ko_ref[kh, rows, :] = tile.astype(ko_ref.dtype)
                vo_ref[kh, rows, :] = (v[:, kh * LANES:(kh + 1) * LANES] * s).astype(vo_ref.dtype)

    for r0 in range(0, tb, sub):
        chain(r0)


def _weight(w):
    if not isinstance(w, tuple):
        return w, _resident(w.shape)
    arr, idx = w
    nd = arr.ndim - 1
    return arr, pl.BlockSpec((None,) + arr.shape[1:], lambda i: (idx,) + (0,) * nd, pipeline_mode=pl.Buffered(1))


def _layer_call(h, p, mixer_args, tail_args, epi_args, *, mixer, epilogue, seq):
    t, d = h.shape
    tb = min(TOKEN_BLOCK, seq)
    nsteps = t // tb
    row = lambda w: pl.BlockSpec((tb, w), lambda i: (i, 0))
    vec = lambda: pl.BlockSpec((1, d), lambda i: (0, 0))
    heads = lambda n: pl.BlockSpec((n, tb, LANES), lambda i: (0, i, 0))

    args, specs = [], []

    def add(arr, spec):
        args.append(arr)
        specs.append(spec)

    def add_weight(w):
        add(*_weight(w))

    p_all, layer = p
    add(h, row(d))
    add(p_all, pl.BlockSpec((None, tb, p_all.shape[-1]), lambda i: (layer, i, 0)))
    if mixer == "s5":
        y, gmix, dskip, wglu = mixer_args
        add(y, row(d))
        add(gmix, vec())
        add(dskip, vec())
        add_weight(wglu)
    else:
        q, kd, vd, sinks, wo = mixer_args
        blocks_per_tb = tb // ATTN_BLOCK
        prev = lambda: pl.BlockSpec((N_KV_HEADS, ATTN_BLOCK, LANES),
                                    lambda i: (0, jnp.maximum(i * blocks_per_tb - 1, 0), 0))
        add(q, heads(d // LANES))
        add(kd, heads(N_KV_HEADS))
        add(kd, prev())
        add(vd, heads(N_KV_HEADS))
        add(vd, prev())
        add(sinks, pl.BlockSpec(memory_space=pltpu.SMEM))
        add_weight(wo)
    for w in tail_args:
        add_weight(w)

    out_shapes, out_specs = [], []
    if epilogue != "final":
        out_shapes.append(jax.ShapeDtypeStruct((t, d), F32))
        out_specs.append(row(d))
    if epilogue in ("norm", "final"):
        add(epi_args[0], vec())
        out_shapes.append(jax.ShapeDtypeStruct((t, d), F32))
        out_specs.append(row(d))
    else:
        add_weight(epi_args[0])
        add(epi_args[1], pl.BlockSpec((1, LANES), lambda i: (0, 0)))
        out_shapes.append(jax.ShapeDtypeStruct((d // LANES, t, LANES), BF16))
        out_specs.append(heads(d // LANES))
        if epilogue == "kvq":
            add_weight(epi_args[2])
            add_weight(epi_args[3])
            out_shapes += [jax.ShapeDtypeStruct((N_KV_HEADS, t, LANES), BF16)] * 2
            out_specs += [heads(N_KV_HEADS), heads(N_KV_HEADS)]

    scratch = [pltpu.VMEM((tb, d), BF16)] if mixer == "attn" else []
    kern = functools.partial(_layer_kernel, mixer=mixer, epilogue=epilogue, tb=tb, seq=seq)
    return pl.pallas_call(
        kern,
        grid=(nsteps,),
        in_specs=specs,
        out_specs=out_specs,
        out_shape=out_shapes,
        scratch_shapes=scratch,
        compiler_params=pltpu.CompilerParams(
            dimension_semantics=("arbitrary",), vmem_limit_bytes=VMEM_LIMIT_BYTES),
        name=f"layer_{mixer}_{epilogue}",
    )(*args)


def kernel(x, p, norm_mix, ssm_lambda_re, ssm_lambda_im, ssm_log_dt, ssm_b_re, ssm_b_im, ssm_c_re, ssm_c_im, ssm_d, ssm_w_glu, kv_norm, w_k, w_v, w_q, attn_sinks, w_o, norm_mlp, w_up, w_down, norm_ple, w_ple_gate, w_ple_proj, norm_final):
    bsz, seq, d = x.shape
    depth = p.shape[0]
    n_a = ssm_lambda_re.shape[0]
    t = bsz * seq
    L = S5_CHUNK
    assert seq % (L * SUBLANES) == 0 and seq % ATTN_BLOCK == 0 and d % LANES == 0

    vec = lambda g: g.reshape(1, d)
    inv = ROPE_THETA ** (-jnp.arange(0, ROT_DIM, 2, dtype=F32) / ROT_DIM)
    inv64 = jnp.concatenate([inv, inv, jnp.zeros((HEAD_DIM - ROT_DIM,), F32)])
    inv_lanes = jnp.tile(inv64, LANES // HEAD_DIM).reshape(1, LANES)
    dup = lambda w: jnp.repeat(w.reshape(d, N_KV_HEADS, 1, HEAD_DIM), 2, axis=2).reshape(d, 2 * N_KV_HEADS * HEAD_DIM)

    p_all = p.reshape(depth, t, p.shape[-1])
    n_glu = d // GLU_TILE
    glu_bf = jnp.transpose(ssm_w_glu.reshape(n_a, d, 2, n_glu, GLU_TILE), (0, 1, 3, 2, 4)).reshape(n_a, d, 2 * d).astype(BF16)
    up_bf = (norm_mlp[:, :, None] * w_up).astype(BF16)
    down_bf = w_down.astype(BF16)
    gate_bf = (norm_ple[:, :, None] * w_ple_gate).astype(BF16)
    proj_bf = w_ple_proj.astype(BF16)
    q_bf = (norm_mix[n_a:, :, None] * w_q).astype(BF16)
    o_bf = w_o.astype(BF16)
    k_bf = dup(kv_norm[:, None] * w_k).astype(BF16)
    v_bf = dup(kv_norm[:, None] * w_v).astype(BF16)

    h = x.reshape(t, d)
    hn = _prenorm(h, vec(norm_mix[0]))
    q = kd = vd = None
    out = None
    for i in range(depth):
        tail = ((up_bf, i), (down_bf, i), (gate_bf, i), (proj_bf, i))
        if i == depth - 1:
            epilogue, epi = "final", (vec(norm_final),)
        elif i + 1 < n_a:
            epilogue, epi = "norm", (vec(norm_mix[i + 1]),)
        elif i + 1 == n_a:
            epilogue, epi = "kvq", ((q_bf, 0), inv_lanes, k_bf, v_bf)
        else:
            epilogue, epi = "q", ((q_bf, i + 1 - n_a), inv_lanes)

        if i < n_a:
            kc, pb, qb, sc = _s5_tables(ssm_lambda_re[i], ssm_lambda_im[i], ssm_log_dt[i],
                                        ssm_b_re[i], ssm_b_im[i], ssm_c_re[i], ssm_c_im[i])
            y = _s5_core(hn, kc, pb, qb, sc, chunks_per_seq=seq // L)
            mixer, mix = "s5", (y, vec(norm_mix[i]), vec(ssm_d[i]), (glu_bf, i))
        else:
            j = i - n_a
            mixer, mix = "attn", (q, kd, vd, attn_sinks[j], (o_bf, j))

        res = _layer_call(h, (p_all, i), mix, tail, epi, mixer=mixer, epilogue=epilogue, seq=seq)
        if epilogue == "final":
            out = res[0]
        elif epilogue == "norm":
            h, hn = res
        elif epilogue == "q":
            h, q = res
        else:
            h, q, kd, vd = res
    return out.reshape(bsz, seq, d)
```

```python
import functools
import math

import jax
import jax.numpy as jnp
from jax import lax
from jax.experimental import pallas as pl
from jax.experimental.pallas import tpu as pltpu

SSM_GROUP = 16
SSM_STATE = 64
HEAD_DIM = 64
N_KV_HEADS = 4
GQA_GROUP = 4
ATTN_BLOCK = 128
WINDOW = 128
ROPE_THETA = 500000.0
ROT_DIM = 16
RMS_EPS = 1e-6
NEG_INF = -1e30

LANES = 128
SUBLANES = 8
VMEM_LIMIT_BYTES = 56 * 1024 * 1024

S5_CHUNK = 16
S5_ROWS = 512
TOKEN_BLOCK = 512
SUB_BLOCK = 256
FF_CHUNK = 1024
GLU_TILE = 256

BF16 = jnp.bfloat16
F32 = jnp.float32


def _dot(a, b):
    return jnp.dot(a, b, preferred_element_type=F32)


def _inv_rms(x):
    return lax.rsqrt(jnp.mean(x * x, axis=-1, keepdims=True) + RMS_EPS)


def _rms(x, g):
    return x * _inv_rms(x) * g


def _gelu_tanh(x):
    c = math.sqrt(2.0 / math.pi)
    return 0.5 * x * (1.0 + jnp.tanh(c * (x + 0.044715 * (x * x * x))))


def _resident(shape):
    nd = len(shape)
    return pl.BlockSpec(shape, lambda i: (0,) * nd, pipeline_mode=pl.Buffered(1))


def _s5_tables(lam_re, lam_im, log_dt, b_re, b_im, c_re, c_im):
    hp = lax.Precision.HIGHEST
    L = S5_CHUNK
    G, N = lam_re.shape
    H = b_re.shape[-1]
    dt = jnp.exp(log_dt)[:, None]
    lr, ph = lam_re * dt, lam_im * dt
    d = jnp.arange(L + 1, dtype=F32)[:, None, None]
    pw_r = jnp.exp(d * lr) * jnp.cos(d * ph)
    pw_i = jnp.exp(d * lr) * jnp.sin(d * ph)
    a_r, a_i = pw_r[1], pw_i[1]
    den = lam_re * lam_re + lam_im * lam_im
    nr = a_r - 1.0
    coef_r = (nr * lam_re + a_i * lam_im) / den
    coef_i = (a_i * lam_re - nr * lam_im) / den
    bb_r = coef_r[..., None] * b_re - coef_i[..., None] * b_im
    bb_i = coef_r[..., None] * b_im + coef_i[..., None] * b_re
    ab_r = pw_r[:L, :, :, None] * bb_r - pw_i[:L, :, :, None] * bb_i
    ab_i = pw_r[:L, :, :, None] * bb_i + pw_i[:L, :, :, None] * bb_r
    kc = (jnp.einsum('dgnh,gkn->ghdk', ab_r, c_re, precision=hp)
          - jnp.einsum('dgnh,gkn->ghdk', ab_i, c_im, precision=hp)).reshape(G, H, L * H)

    p_r = jnp.transpose(ab_r[::-1], (1, 0, 3, 2)).reshape(G // 2, 2, L * H, N)
    p_i = jnp.transpose(ab_i[::-1], (1, 0, 3, 2)).reshape(G // 2, 2, L * H, N)
    zp = jnp.zeros_like(p_r[:, 0])
    pb = jnp.concatenate([jnp.concatenate([p_r[:, 0], zp, p_i[:, 0], zp], axis=-1),
                          jnp.concatenate([zp, p_r[:, 1], zp, p_i[:, 1]], axis=-1)], axis=1)

    ca_r = c_re[None] * pw_r[1:, :, None, :] - c_im[None] * pw_i[1:, :, None, :]
    ca_i = c_re[None] * pw_i[1:, :, None, :] + c_im[None] * pw_r[1:, :, None, :]
    q_r = jnp.transpose(ca_r, (1, 3, 0, 2)).reshape(G // 2, 2, N, L * H)
    q_i = jnp.transpose(-ca_i, (1, 3, 0, 2)).reshape(G // 2, 2, N, L * H)
    zq = jnp.zeros_like(q_r[:, 0])
    qb = jnp.concatenate([jnp.concatenate([q_r[:, 0], zq], axis=-1), jnp.concatenate([zq, q_r[:, 1]], axis=-1),
                          jnp.concatenate([q_i[:, 0], zq], axis=-1), jnp.concatenate([zq, q_i[:, 1]], axis=-1)],
                         axis=1)

    mm = jnp.array([1, 2, 3, 4, 5, 6, 7, 8, 1, 2, 4, 8, 0, 0, 0, 0], F32)[:, None, None] * L
    sc = jnp.stack([jnp.exp(mm * lr) * jnp.cos(mm * ph), jnp.exp(mm * lr) * jnp.sin(mm * ph)])
    sc = jnp.transpose(sc.reshape(2, 16, G // 2, 2 * N), (2, 0, 1, 3))
    return kc, pb.astype(BF16), qb.astype(BF16), sc


def _block_transpose(vs, lane_blk):
    vs = list(vs)
    for d in (4, 2, 1):
        take = (lane_blk & d) != 0
        for r in range(len(vs)):
            if r & d:
                continue
            top, bot = vs[r], vs[r + d]
            if 2 * d * SSM_GROUP == LANES:
                swapped = pltpu.roll(jnp.where(take, top, bot), d * SSM_GROUP, axis=1)
                vs[r] = jnp.where(take, swapped, top)
                vs[r + d] = jnp.where(take, bot, swapped)
            else:
                vs[r] = jnp.where(take, pltpu.roll(bot, d * SSM_GROUP, axis=1), top)
                vs[r + d] = jnp.where(take, bot, pltpu.roll(top, LANES - d * SSM_GROUP, axis=1))
    return vs


def _toeplitz_rows(kc, j, lane):
    lo, hi = kc[:, :LANES], kc[:, LANES:]
    r = (j % SUBLANES) * SSM_GROUP
    if r:
        lo_s, hi_s = pltpu.roll(lo, r, axis=1), pltpu.roll(hi, r, axis=1)
        first = jnp.where(lane < r, 0.0, lo_s)
        second = jnp.where(lane < r, lo_s, hi_s)
    else:
        first, second = lo, hi
    if j < SUBLANES:
        return jnp.concatenate([first, second], axis=1)
    return jnp.concatenate([jnp.zeros_like(first), first], axis=1)


def _s5_core_kernel(*refs, rows, tiles_per_seq):
    L = S5_CHUNK
    hn_refs = refs[:L]
    kc_ref, p_ref, q_ref, sc_ref, y_ref, t_scr, u_scr, v_scr, x_scr, yg_scr, carry_scr = refs[L:]
    n_pair = p_ref.shape[0]
    kk = t_scr.shape[1]
    rb = pl.program_id(1)
    lane_blk = lax.broadcasted_iota(jnp.int32, (rows, LANES), 1) // SSM_GROUP

    @pl.when(rb == 0)
    def _():
        carry_scr[...] = jnp.zeros_like(carry_scr)
        lane = lax.broadcasted_iota(jnp.int32, (SSM_GROUP, LANES), 1)
        for g in range(2 * n_pair):
            kc = kc_ref[g]
            for j in range(L):
                t_scr[g, j * SSM_GROUP:(j + 1) * SSM_GROUP, :] = _toeplitz_rows(kc, j, lane).astype(BF16)

    for jh in range(L // SUBLANES):
        vs = [hn_refs[jh * SUBLANES + r][...].astype(BF16) for r in range(SUBLANES)]
        vs = _block_transpose(vs, lane_blk)
        for gl in range(2 * n_pair):
            col = ((gl % 2) * (L // SUBLANES) + jh) * LANES
            u_scr[gl // 2, :, col:col + LANES] = vs[gl]

    for gp in range(n_pair):
        v_scr[gp] = _dot(u_scr[gp], p_ref[gp])

    half = LANES
    row = lax.broadcasted_iota(jnp.int32, (SUBLANES, half), 0)
    tiles = rows // SUBLANES

    def tile_body(t, carry):
        keep = ((rb * tiles + t) % tiles_per_seq) != 0
        r0 = pl.multiple_of(t * SUBLANES, SUBLANES)
        out = []
        for gp in range(n_pair):
            cr = jnp.where(keep, carry[2 * gp], 0.0)
            ci = jnp.where(keep, carry[2 * gp + 1], 0.0)
            r = v_scr[gp, pl.ds(r0, SUBLANES), 0:half]
            i = v_scr[gp, pl.ds(r0, SUBLANES), half:2 * half]
            for s_idx, s in enumerate((1, 2, 4)):
                ar = sc_ref[gp, 0, 8 + s_idx:9 + s_idx, :]
                ai = sc_ref[gp, 1, 8 + s_idx:9 + s_idx, :]
                rs = jnp.where(row >= s, pltpu.roll(r, s, axis=0), 0.0)
                is_ = jnp.where(row >= s, pltpu.roll(i, s, axis=0), 0.0)
                r, i = r + (ar * rs - ai * is_), i + (ar * is_ + ai * rs)
            pw_r = sc_ref[gp, 0, 0:SUBLANES, :]
            pw_i = sc_ref[gp, 1, 0:SUBLANES, :]
            inc_r = r + (pw_r * cr - pw_i * ci)
            inc_i = i + (pw_r * ci + pw_i * cr)
            x_scr[gp, pl.ds(r0, SUBLANES), 0:half] = jnp.where(row == 0, cr, pltpu.roll(inc_r, 1, axis=0))
            x_scr[gp, pl.ds(r0, SUBLANES), half:2 * half] = jnp.where(row == 0, ci, pltpu.roll(inc_i, 1, axis=0))
            last_r = jnp.broadcast_to(r[SUBLANES - 1:SUBLANES, :], (SUBLANES, half))
            last_i = jnp.broadcast_to(i[SUBLANES - 1:SUBLANES, :], (SUBLANES, half))
            a8r = sc_ref[gp, 0, 11:12, :]
            a8i = sc_ref[gp, 1, 11:12, :]
            out += [last_r + (a8r * cr - a8i * ci), last_i + (a8r * ci + a8i * cr)]
        return tuple(out)

    init = tuple(carry_scr[k] for k in range(2 * n_pair))
    fin = lax.fori_loop(0, tiles, tile_body, init, unroll=2)
    for k in range(2 * n_pair):
        carry_scr[k] = fin[k]

    for gp in range(n_pair):
        carry_y = _dot(x_scr[gp].astype(BF16), q_ref[gp])
        for g2 in range(2):
            cols = slice(g2 * kk, (g2 + 1) * kk)
            yg_scr[2 * gp + g2] = _dot(u_scr[gp, :, cols], t_scr[2 * gp + g2]) + carry_y[:, cols]

    for ih in range(L // SUBLANES):
        vs = [yg_scr[gl, :, ih * LANES:(ih + 1) * LANES] for gl in range(2 * n_pair)]
        vs = _block_transpose(vs, lane_blk)
        for r in range(SUBLANES):
            y_ref[pl.ds(ih * SUBLANES + r, rows, stride=L), :] = vs[r]


def _s5_core(hn, kc, pb, qb, sc, *, layer, chunks_per_seq):
    tokens, d = hn.shape
    L = S5_CHUNK
    m = tokens // L
    rows = min(S5_ROWS, m)
    kk = L * SSM_GROUP
    n_pair = LANES // (2 * SSM_GROUP)
    n_tiles = d // LANES
    kern = functools.partial(_s5_core_kernel, rows=rows, tiles_per_seq=chunks_per_seq // SUBLANES)
    per_tile = lambda *shape: pl.BlockSpec(shape, lambda lt, rb: (layer * n_tiles + lt,) + (0,) * (len(shape) - 1))
    hn_chunks = hn.reshape(m, L * d)
    plane = lambda j: pl.BlockSpec((rows, LANES), lambda lt, rb: (rb, j * n_tiles + lt))
    return pl.pallas_call(
        kern,
        grid=(n_tiles, m // rows),
        in_specs=[plane(j) for j in range(L)] + [
            per_tile(2 * n_pair, SSM_GROUP, kk),
            per_tile(n_pair, 2 * kk, kk),
            per_tile(n_pair, kk, 2 * kk),
            per_tile(n_pair, 2, 16, LANES),
        ],
        out_specs=pl.BlockSpec((rows * L, LANES), lambda lt, rb: (rb, lt)),
        out_shape=jax.ShapeDtypeStruct((tokens, d), F32),
        scratch_shapes=[
            pltpu.VMEM((2 * n_pair, kk, kk), BF16),
            pltpu.VMEM((n_pair, rows, 2 * kk), BF16),
            pltpu.VMEM((n_pair, rows, 2 * LANES), F32),
            pltpu.VMEM((n_pair, rows, 2 * LANES), F32),
            pltpu.VMEM((2 * n_pair, rows, kk), F32),
            pltpu.VMEM((2 * n_pair, SUBLANES, LANES), F32),
        ],
        compiler_params=pltpu.CompilerParams(
            dimension_semantics=("arbitrary", "arbitrary"), vmem_limit_bytes=VMEM_LIMIT_BYTES),
        name="s5_core",
    )(*([hn_chunks] * L), kc, pb, qb, sc)


def _prenorm_kernel(x_ref, g_ref, o_ref):
    o_ref[...] = _rms(x_ref[...], g_ref[...]).astype(o_ref.dtype)


def _prenorm(x, g):
    t, d = x.shape
    tb = min(t, 1024)
    return pl.pallas_call(
        _prenorm_kernel,
        grid=(t // tb,),
        in_specs=[pl.BlockSpec((tb, d), lambda i: (i, 0)), pl.BlockSpec((1, d), lambda i: (0, 0))],
        out_specs=pl.BlockSpec((tb, d), lambda i: (i, 0)),
        out_shape=jax.ShapeDtypeStruct((t, d), F32),
        compiler_params=pltpu.CompilerParams(dimension_semantics=("arbitrary",)),
        name="prenorm",
    )(x, g)


def _rope(x, cos_t, sin_t, l64):
    half = ROT_DIM // 2
    lo = pltpu.roll(x, half, axis=1)
    hi = pltpu.roll(x, LANES - half, axis=1)
    return x * cos_t + jnp.where(l64 < half, -hi, lo) * sin_t


def _rope_tables(inv_ref, row0, nrows, seq):
    pos = (row0 + lax.broadcasted_iota(jnp.int32, (nrows, LANES), 0)) % seq
    ang = pos.astype(F32) * inv_ref[...]
    l64 = lax.broadcasted_iota(jnp.int32, (nrows, LANES), 1) % HEAD_DIM
    return jnp.cos(ang), jnp.sin(ang), l64


def _attention_consts():
    nq = ATTN_BLOCK
    ii = lax.broadcasted_iota(jnp.int32, (nq, 2 * nq), 0)
    jj = lax.broadcasted_iota(jnp.int32, (nq, 2 * nq), 1) % nq
    cur = jj <= ii
    lane = lax.broadcasted_iota(jnp.int32, (nq, LANES), 1)
    first = lane < HEAD_DIM
    return dict(ii=ii, jj=jj, cur=cur, first=first, zero=jnp.zeros((nq, LANES), BF16),
                cur_bf=jnp.where(cur, 1.0, 0.0).astype(BF16), prev_bf=jnp.where(cur, 0.0, 1.0).astype(BF16),
                ones_a=jnp.where(first, 1.0, 0.0).astype(BF16), ones_b=jnp.where(first, 0.0, 1.0).astype(BF16))


def _kv_blocks(k, v, c):
    first, zero = c["first"], c["zero"]
    k_blk = jnp.concatenate([jnp.where(first, k, zero), jnp.where(first, zero, k)], axis=0)
    v_blk = jnp.concatenate([
        jnp.concatenate([jnp.where(first, v, zero), c["ones_a"]], axis=1),
        jnp.concatenate([jnp.where(first, zero, v), c["ones_b"]], axis=1)], axis=0)
    return k_blk, v_blk


def _attention(q_ref, kc_ref, kp_ref, vc_ref, vp_ref, sink_ref, o_scr, c, *, row0, nrows, tb, seq):
    nq = ATTN_BLOCK
    pairs = GQA_GROUP // 2
    nt = (((1,), (1,)), ((), ()))
    blocks = {}

    def kv(kh, b):
        if (kh, b) not in blocks:
            if b < 0:
                blocks[kh, b] = _kv_blocks(kp_ref[kh], vp_ref[kh], c)
            else:
                rows = slice(b * nq, (b + 1) * nq)
                blocks[kh, b] = _kv_blocks(kc_ref[kh, rows, :], vc_ref[kh, rows, :], c)
        return blocks[kh, b]

    for b in range(row0 // nq, (row0 + nrows) // nq):
        rows = slice(b * nq, (b + 1) * nq)
        has_prev = ((pl.program_id(0) * (tb // nq) + b) % (seq // nq)) != 0
        valid = c["jj"] <= c["ii"] + jnp.where(has_prev, nq, 0)
        for kh in range(N_KV_HEADS):
            k_cur, v_cur = kv(kh, b)
            k_prev, v_prev = kv(kh, b - 1)
            for pair in range(pairs):
                hp = kh * pairs + pair
                qp = q_ref[hp, rows, :]
                s_cur = lax.dot_general(qp, k_cur, nt, preferred_element_type=F32)
                s_prev = lax.dot_general(qp, k_prev, nt, preferred_element_type=F32)
                s = jnp.where(valid, jnp.where(c["cur"], s_cur, s_prev), NEG_INF)
                es, sinks = [], []
                for hh in range(2):
                    sh = s[:, hh * nq:(hh + 1) * nq]
                    sink = sink_ref[2 * hp + hh]
                    mx = jnp.maximum(jnp.max(sh, axis=-1, keepdims=True), sink)
                    es.append(jnp.exp(sh - mx))
                    sinks.append(jnp.exp(sink - mx))
                e = jnp.concatenate(es, axis=1).astype(BF16)
                o = _dot(e * c["cur_bf"], v_cur) + _dot(e * c["prev_bf"], v_prev)
                den = o[:, LANES:] + jnp.where(c["first"], sinks[0], sinks[1])
                o_scr[rows, hp * LANES:(hp + 1) * LANES] = (o[:, :LANES] / den).astype(o_scr.dtype)


def _layer_kernel(*refs, mixer, epilogue, tb, seq):
    it = iter(refs)
    h_ref = next(it)
    p_ref = next(it)
    if mixer == "s5":
        y_ref, gmix_ref, d_ref, wglu_ref = next(it), next(it), next(it), next(it)
    else:
        q_ref, kc_ref, kp_ref, vc_ref, vp_ref, sink_ref, wo_ref = (next(it) for _ in range(7))
    wup_ref, wdown_ref, wgate_ref, wproj_ref = (next(it) for _ in range(4))
    if epilogue in ("norm", "final"):
        gnext_ref = next(it)
    else:
        wq_ref, inv_ref = next(it), next(it)
        if epilogue == "kvq":
            wk_ref, wv_ref = next(it), next(it)
    ho_ref = None if epilogue == "final" else next(it)
    if epilogue == "norm":
        hn_ref = next(it)
    elif epilogue == "q":
        qo_ref = next(it)
    elif epilogue == "kvq":
        qo_ref, ko_ref, vo_ref = next(it), next(it), next(it)
    elif epilogue == "final":
        out_ref = next(it)
    if mixer == "attn":
        o_scr = next(it)

    d_model = h_ref.shape[-1]
    d_ff = wup_ref.shape[1]
    sub = min(SUB_BLOCK, tb)
    consts = _attention_consts() if mixer == "attn" else None

    def chain(r0):
        rows = slice(r0, r0 + sub)
        h = h_ref[rows, :]

        if mixer == "s5":
            u = _rms(h, gmix_ref[...])
            ge = _gelu_tanh(y_ref[rows, :] + d_ref[...] * u).astype(BF16)
            ab = _dot(ge, wglu_ref[...])
            mix = [ab[:, (2 * k) * GLU_TILE:(2 * k + 1) * GLU_TILE]
                   * jax.nn.sigmoid(ab[:, (2 * k + 1) * GLU_TILE:(2 * k + 2) * GLU_TILE])
                   for k in range(d_model // GLU_TILE)]
            h = h + jnp.concatenate(mix, axis=1)
        else:
            _attention(q_ref, kc_ref, kp_ref, vc_ref, vp_ref, sink_ref, o_scr, consts,
                       row0=r0, nrows=sub, tb=tb, seq=seq)
            h = h + _dot(o_scr[rows, :], wo_ref[...])

        s = _inv_rms(h)
        hb = h.astype(BF16)
        acc = None
        for c0 in range(0, d_ff, FF_CHUNK):
            up = _dot(hb, wup_ref[:, c0:c0 + FF_CHUNK])
            act = jnp.square(jnp.maximum(up, 0.0)).astype(BF16)
            part = _dot(act, wdown_ref[c0:c0 + FF_CHUNK, :])
            acc = part if acc is None else acc + part
        h = h + (s * s) * acc

        gate = _inv_rms(h) * _dot(h.astype(BF16), wgate_ref[...])
        proj = _dot(p_ref[rows, :].astype(BF16), wproj_ref[...])
        h = h + jax.nn.sigmoid(gate) * proj

        if epilogue == "final":
            out_ref[rows, :] = _rms(h, gnext_ref[...])
            return
        ho_ref[rows, :] = h
        if epilogue == "norm":
            hn_ref[rows, :] = _rms(h, gnext_ref[...]).astype(hn_ref.dtype)
            return
        cos_t, sin_t, l64 = _rope_tables(inv_ref, pl.program_id(0) * tb + r0, sub, seq)
        hb = h.astype(BF16)
        s = _inv_rms(h)
        q = _dot(hb, wq_ref[...])
        if epilogue == "kvq":
            k = _dot(hb, wk_ref[...])
            v = _dot(hb, wv_ref[...])
        qs = s * HEAD_DIM ** -0.5
        for hp in range(d_model // LANES):
            tile = _rope(q[:, hp * LANES:(hp + 1) * LANES] * qs, cos_t, sin_t, l64)
            qo_ref[hp, rows, :] = tile.astype(qo_ref.dtype)
        if epilogue == "kvq":
            for kh in range(N_KV_HEADS):
                tile = _rope(k[:, kh * LANES:(kh + 1) * LANES] * s, cos_t, sin_t, l64)
                ko_ref[kh, rows, :] = tile.astype(ko_ref.dtype)
                vo_ref[kh, rows, :] = (v[:, kh * LANES:(kh + 1) * LANES] * s).astype(vo_ref.dtype)

    for r0 in range(0, tb, sub):
        chain(r0)


def _weight(w):
    if not isinstance(w, tuple):
        return w, _resident(w.shape)
    arr, idx = w
    nd = arr.ndim - 1
    return arr, pl.BlockSpec((None,) + arr.shape[1:], lambda i: (idx,) + (0,) * nd, pipeline_mode=pl.Buffered(1))


def _layer_call(h, p, mixer_args, tail_args, epi_args, *, mixer, epilogue, seq):
    t, d = h.shape
    tb = min(TOKEN_BLOCK, seq)
    nsteps = t // tb
    row = lambda w: pl.BlockSpec((tb, w), lambda i: (i, 0))
    vec = lambda: pl.BlockSpec((1, d), lambda i: (0, 0))
    heads = lambda n: pl.BlockSpec((n, tb, LANES), lambda i: (0, i, 0))

    args, specs = [], []

    def add(arr, spec):
        args.append(arr)
        specs.append(spec)

    def add_weight(w):
        add(*_weight(w))

    p_all, layer = p
    add(h, row(d))
    add(p_all, pl.BlockSpec((None, tb, p_all.shape[-1]), lambda i: (layer, i, 0)))
    if mixer == "s5":
        y, gmix, dskip, wglu = mixer_args
        add(y, row(d))
        add(gmix, vec())
        add(dskip, vec())
        add_weight(wglu)
    else:
        q, kd, vd, sinks, wo = mixer_args
        blocks_per_tb = tb // ATTN_BLOCK
        prev = lambda: pl.BlockSpec((N_KV_HEADS, ATTN_BLOCK, LANES),
                                    lambda i: (0, jnp.maximum(i * blocks_per_tb - 1, 0), 0))
        add(q, heads(d // LANES))
        add(kd, heads(N_KV_HEADS))
        add(kd, prev())
        add(vd, heads(N_KV_HEADS))
        add(vd, prev())
        add(sinks, pl.BlockSpec(memory_space=pltpu.SMEM))
        add_weight(wo)
    for w in tail_args:
        add_weight(w)

    out_shapes, out_specs = [], []
    if epilogue != "final":
        out_shapes.append(jax.ShapeDtypeStruct((t, d), F32))
        out_specs.append(row(d))
    if epilogue in ("norm", "final"):
        add(epi_args[0], vec())
        out_shapes.append(jax.ShapeDtypeStruct((t, d), F32))
        out_specs.append(row(d))
    else:
        add_weight(epi_args[0])
        add(epi_args[1], pl.BlockSpec((1, LANES), lambda i: (0, 0)))
        out_shapes.append(jax.ShapeDtypeStruct((d // LANES, t, LANES), BF16))
        out_specs.append(heads(d // LANES))
        if epilogue == "kvq":
            add_weight(epi_args[2])
            add_weight(epi_args[3])
            out_shapes += [jax.ShapeDtypeStruct((N_KV_HEADS, t, LANES), BF16)] * 2
            out_specs += [heads(N_KV_HEADS), heads(N_KV_HEADS)]

    scratch = [pltpu.VMEM((tb, d), BF16)] if mixer == "attn" else []
    kern = functools.partial(_layer_kernel, mixer=mixer, epilogue=epilogue, tb=tb, seq=seq)
    return pl.pallas_call(
        kern,
        grid=(nsteps,),
        in_specs=specs,
        out_specs=out_specs,
        out_shape=out_shapes,
        scratch_shapes=scratch,
        compiler_params=pltpu.CompilerParams(
            dimension_semantics=("arbitrary",), vmem_limit_bytes=VMEM_LIMIT_BYTES),
        name=f"layer_{mixer}_{epilogue}",
    )(*args)


def kernel(x, p, norm_mix, ssm_lambda_re, ssm_lambda_im, ssm_log_dt, ssm_b_re, ssm_b_im, ssm_c_re, ssm_c_im, ssm_d, ssm_w_glu, kv_norm, w_k, w_v, w_q, attn_sinks, w_o, norm_mlp, w_up, w_down, norm_ple, w_ple_gate, w_ple_proj, norm_final):
    bsz, seq, d = x.shape
    depth = p.shape[0]
    n_a = ssm_lambda_re.shape[0]
    t = bsz * seq
    L = S5_CHUNK
    assert seq % (L * SUBLANES) == 0 and seq % ATTN_BLOCK == 0 and d % LANES == 0

    vec = lambda g: g.reshape(1, d)
    inv = ROPE_THETA ** (-jnp.arange(0, ROT_DIM, 2, dtype=F32) / ROT_DIM)
    inv64 = jnp.concatenate([inv, inv, jnp.zeros((HEAD_DIM - ROT_DIM,), F32)])
    inv_lanes = jnp.tile(inv64, LANES // HEAD_DIM).reshape(1, LANES)
    dup = lambda w: jnp.repeat(w.reshape(d, N_KV_HEADS, 1, HEAD_DIM), 2, axis=2).reshape(d, 2 * N_KV_HEADS * HEAD_DIM)

    p_all = p.reshape(depth, t, p.shape[-1])
    n_glu = d // GLU_TILE
    glu_bf = jnp.transpose(ssm_w_glu.reshape(n_a, d, 2, n_glu, GLU_TILE), (0, 1, 3, 2, 4)).reshape(n_a, d, 2 * d).astype(BF16)
    up_bf = (norm_mlp[:, :, None] * w_up).astype(BF16)
    down_bf = w_down.astype(BF16)
    gate_bf = (norm_ple[:, :, None] * w_ple_gate).astype(BF16)
    proj_bf = w_ple_proj.astype(BF16)
    q_bf = (norm_mix[n_a:, :, None] * w_q).astype(BF16)
    o_bf = w_o.astype(BF16)
    k_bf = dup(kv_norm[:, None] * w_k).astype(BF16)
    v_bf = dup(kv_norm[:, None] * w_v).astype(BF16)

    merge = lambda a: a.reshape((-1,) + a.shape[2:])
    tables = _s5_tables(merge(ssm_lambda_re), merge(ssm_lambda_im), merge(ssm_log_dt), merge(ssm_b_re),
                        merge(ssm_b_im), merge(ssm_c_re), merge(ssm_c_im))

    h = x.reshape(t, d)
    hn = _prenorm(h, vec(norm_mix[0]))
    q = kd = vd = None
    out = None
    for i in range(depth):
        tail = ((up_bf, i), (down_bf, i), (gate_bf, i), (proj_bf, i))
        if i == depth - 1:
            epilogue, epi = "final", (vec(norm_final),)
        elif i + 1 < n_a:
            epilogue, epi = "norm", (vec(norm_mix[i + 1]),)
        elif i + 1 == n_a:
            epilogue, epi = "kvq", ((q_bf, 0), inv_lanes, k_bf, v_bf)
        else:
            epilogue, epi = "q", ((q_bf, i + 1 - n_a), inv_lanes)

        if i < n_a:
            y = _s5_core(hn, *tables, layer=i, chunks_per_seq=seq // L)
            mixer, mix = "s5", (y, vec(norm_mix[i]), vec(ssm_d[i]), (glu_bf, i))
        else:
            j = i - n_a
            mixer, mix = "attn", (q, kd, vd, attn_sinks[j], (o_bf, j))

        res = _layer_call(h, (p_all, i), mix, tail, epi, mixer=mixer, epilogue=epilogue, seq=seq)
        if epilogue == "final":
            out = res[0]
        elif epilogue == "norm":
            h, hn = res
        elif epilogue == "q":
            h, q = res
        else:
            h, q, kd, vd = res
    return out.reshape(bsz, seq, d)
```

```python
import functools
import math

import jax
import jax.numpy as jnp
from jax import lax
from jax.experimental import pallas as pl
from jax.experimental.pallas import tpu as pltpu

SSM_GROUP = 16
SSM_STATE = 64
HEAD_DIM = 64
N_KV_HEADS = 4
GQA_GROUP = 4
ATTN_BLOCK = 128
WINDOW = 128
ROPE_THETA = 500000.0
ROT_DIM = 16
RMS_EPS = 1e-6
NEG_INF = -1e30

LANES = 128
SUBLANES = 8
VMEM_LIMIT_BYTES = 56 * 1024 * 1024

S5_CHUNK = 16
S5_ROWS = 512
TOKEN_BLOCK = 512
SUB_BLOCK = 256
FF_CHUNK = 1024
GLU_TILE = 256

BF16 = jnp.bfloat16
F32 = jnp.float32


def _dot(a, b):
    return jnp.dot(a, b, preferred_element_type=F32)


def _inv_rms(x):
    return lax.rsqrt(jnp.mean(x * x, axis=-1, keepdims=True) + RMS_EPS)


def _rms(x, g):
    return x * _inv_rms(x) * g


def _gelu_tanh(x):
    c = math.sqrt(2.0 / math.pi)
    return 0.5 * x * (1.0 + jnp.tanh(c * (x + 0.044715 * (x * x * x))))


def _resident(shape):
    nd = len(shape)
    return pl.BlockSpec(shape, lambda i: (0,) * nd, pipeline_mode=pl.Buffered(1))


def _s5_tables(lam_re, lam_im, log_dt, b_re, b_im, c_re, c_im):
    hp = lax.Precision.HIGHEST
    L = S5_CHUNK
    G, N = lam_re.shape
    H = b_re.shape[-1]
    dt = jnp.exp(log_dt)[:, None]
    lr, ph = lam_re * dt, lam_im * dt
    d = jnp.arange(L + 1, dtype=F32)[:, None, None]
    pw_r = jnp.exp(d * lr) * jnp.cos(d * ph)
    pw_i = jnp.exp(d * lr) * jnp.sin(d * ph)
    a_r, a_i = pw_r[1], pw_i[1]
    den = lam_re * lam_re + lam_im * lam_im
    nr = a_r - 1.0
    coef_r = (nr * lam_re + a_i * lam_im) / den
    coef_i = (a_i * lam_re - nr * lam_im) / den
    bb_r = coef_r[..., None] * b_re - coef_i[..., None] * b_im
    bb_i = coef_r[..., None] * b_im + coef_i[..., None] * b_re
    ab_r = pw_r[:L, :, :, None] * bb_r - pw_i[:L, :, :, None] * bb_i
    ab_i = pw_r[:L, :, :, None] * bb_i + pw_i[:L, :, :, None] * bb_r
    kc = (jnp.einsum('dgnh,gkn->ghdk', ab_r, c_re, precision=hp)
          - jnp.einsum('dgnh,gkn->ghdk', ab_i, c_im, precision=hp)).reshape(G, H, L * H)

    p_r = jnp.transpose(ab_r[::-1], (1, 0, 3, 2)).reshape(G // 2, 2, L * H, N)
    p_i = jnp.transpose(ab_i[::-1], (1, 0, 3, 2)).reshape(G // 2, 2, L * H, N)
    zp = jnp.zeros_like(p_r[:, 0])
    pb = jnp.concatenate([jnp.concatenate([p_r[:, 0], zp, p_i[:, 0], zp], axis=-1),
                          jnp.concatenate([zp, p_r[:, 1], zp, p_i[:, 1]], axis=-1)], axis=1)

    ca_r = c_re[None] * pw_r[1:, :, None, :] - c_im[None] * pw_i[1:, :, None, :]
    ca_i = c_re[None] * pw_i[1:, :, None, :] + c_im[None] * pw_r[1:, :, None, :]
    q_r = jnp.transpose(ca_r, (1, 3, 0, 2)).reshape(G // 2, 2, N, L * H)
    q_i = jnp.transpose(-ca_i, (1, 3, 0, 2)).reshape(G // 2, 2, N, L * H)
    zq = jnp.zeros_like(q_r[:, 0])
    qb = jnp.concatenate([jnp.concatenate([q_r[:, 0], zq], axis=-1), jnp.concatenate([zq, q_r[:, 1]], axis=-1),
                          jnp.concatenate([q_i[:, 0], zq], axis=-1), jnp.concatenate([zq, q_i[:, 1]], axis=-1)],
                         axis=1)

    mm = jnp.array([1, 2, 3, 4, 5, 6, 7, 8, 1, 2, 4, 8, 0, 0, 0, 0], F32)[:, None, None] * L
    sc = jnp.stack([jnp.exp(mm * lr) * jnp.cos(mm * ph), jnp.exp(mm * lr) * jnp.sin(mm * ph)])
    sc = jnp.transpose(sc.reshape(2, 16, G // 2, 2 * N), (2, 0, 1, 3))
    return kc, pb.astype(BF16), qb.astype(BF16), sc


def _block_transpose(vs, lane_blk):
    vs = list(vs)
    for d in (4, 2, 1):
        take = (lane_blk & d) != 0
        for r in range(len(vs)):
            if r & d:
                continue
            top, bot = vs[r], vs[r + d]
            if 2 * d * SSM_GROUP == LANES:
                swapped = pltpu.roll(jnp.where(take, top, bot), d * SSM_GROUP, axis=1)
                vs[r] = jnp.where(take, swapped, top)
                vs[r + d] = jnp.where(take, bot, swapped)
            else:
                vs[r] = jnp.where(take, pltpu.roll(bot, d * SSM_GROUP, axis=1), top)
                vs[r + d] = jnp.where(take, bot, pltpu.roll(top, LANES - d * SSM_GROUP, axis=1))
    return vs


def _toeplitz_rows(kc, j, lane):
    lo, hi = kc[:, :LANES], kc[:, LANES:]
    r = (j % SUBLANES) * SSM_GROUP
    if r:
        lo_s, hi_s = pltpu.roll(lo, r, axis=1), pltpu.roll(hi, r, axis=1)
        first = jnp.where(lane < r, 0.0, lo_s)
        second = jnp.where(lane < r, lo_s, hi_s)
    else:
        first, second = lo, hi
    if j < SUBLANES:
        return jnp.concatenate([first, second], axis=1)
    return jnp.concatenate([jnp.zeros_like(first), first], axis=1)


def _s5_core_kernel(*refs, rows, tiles_per_seq):
    L = S5_CHUNK
    hn_ref, kc_ref, p_ref, q_ref, sc_ref, y_ref, t_scr, u_scr, v_scr, x_scr, yg_scr, carry_scr = refs
    n_pair = p_ref.shape[0]
    kk = t_scr.shape[1]
    rb = pl.program_id(1)
    lane_blk = lax.broadcasted_iota(jnp.int32, (rows, LANES), 1) // SSM_GROUP

    @pl.when(rb == 0)
    def _():
        carry_scr[...] = jnp.zeros_like(carry_scr)
        lane = lax.broadcasted_iota(jnp.int32, (SSM_GROUP, LANES), 1)
        for g in range(2 * n_pair):
            kc = kc_ref[g]
            for j in range(L):
                t_scr[g, j * SSM_GROUP:(j + 1) * SSM_GROUP, :] = _toeplitz_rows(kc, j, lane).astype(BF16)

    for jh in range(L // SUBLANES):
        vs = [hn_ref[pl.ds(jh * SUBLANES + r, rows, stride=L), :].astype(BF16) for r in range(SUBLANES)]
        vs = _block_transpose(vs, lane_blk)
        for gl in range(2 * n_pair):
            col = ((gl % 2) * (L // SUBLANES) + jh) * LANES
            u_scr[gl // 2, :, col:col + LANES] = vs[gl]

    for gp in range(n_pair):
        v_scr[gp] = _dot(u_scr[gp], p_ref[gp])

    half = LANES
    row = lax.broadcasted_iota(jnp.int32, (SUBLANES, half), 0)
    tiles = rows // SUBLANES

    def tile_body(t, carry):
        keep = ((rb * tiles + t) % tiles_per_seq) != 0
        r0 = pl.multiple_of(t * SUBLANES, SUBLANES)
        out = []
        for gp in range(n_pair):
            cr = jnp.where(keep, carry[2 * gp], 0.0)
            ci = jnp.where(keep, carry[2 * gp + 1], 0.0)
            r = v_scr[gp, pl.ds(r0, SUBLANES), 0:half]
            i = v_scr[gp, pl.ds(r0, SUBLANES), half:2 * half]
            for s_idx, s in enumerate((1, 2, 4)):
                ar = sc_ref[gp, 0, 8 + s_idx:9 + s_idx, :]
                ai = sc_ref[gp, 1, 8 + s_idx:9 + s_idx, :]
                rs = jnp.where(row >= s, pltpu.roll(r, s, axis=0), 0.0)
                is_ = jnp.where(row >= s, pltpu.roll(i, s, axis=0), 0.0)
                r, i = r + (ar * rs - ai * is_), i + (ar * is_ + ai * rs)
            pw_r = sc_ref[gp, 0, 0:SUBLANES, :]
            pw_i = sc_ref[gp, 1, 0:SUBLANES, :]
            inc_r = r + (pw_r * cr - pw_i * ci)
            inc_i = i + (pw_r * ci + pw_i * cr)
            x_scr[gp, pl.ds(r0, SUBLANES), 0:half] = jnp.where(row == 0, cr, pltpu.roll(inc_r, 1, axis=0))
            x_scr[gp, pl.ds(r0, SUBLANES), half:2 * half] = jnp.where(row == 0, ci, pltpu.roll(inc_i, 1, axis=0))
            last_r = jnp.broadcast_to(r[SUBLANES - 1:SUBLANES, :], (SUBLANES, half))
            last_i = jnp.broadcast_to(i[SUBLANES - 1:SUBLANES, :], (SUBLANES, half))
            a8r = sc_ref[gp, 0, 11:12, :]
            a8i = sc_ref[gp, 1, 11:12, :]
            out += [last_r + (a8r * cr - a8i * ci), last_i + (a8r * ci + a8i * cr)]
        return tuple(out)

    init = tuple(carry_scr[k] for k in range(2 * n_pair))
    fin = lax.fori_loop(0, tiles, tile_body, init, unroll=2)
    for k in range(2 * n_pair):
        carry_scr[k] = fin[k]

    for gp in range(n_pair):
        carry_y = _dot(x_scr[gp].astype(BF16), q_ref[gp])
        for g2 in range(2):
            cols = slice(g2 * kk, (g2 + 1) * kk)
            yg_scr[2 * gp + g2] = _dot(u_scr[gp, :, cols], t_scr[2 * gp + g2]) + carry_y[:, cols]

    for ih in range(L // SUBLANES):
        vs = [yg_scr[gl, :, ih * LANES:(ih + 1) * LANES] for gl in range(2 * n_pair)]
        vs = _block_transpose(vs, lane_blk)
        for r in range(SUBLANES):
            y_ref[pl.ds(ih * SUBLANES + r, rows, stride=L), :] = vs[r]


def _s5_core(hn, kc, pb, qb, sc, *, layer, chunks_per_seq):
    tokens, d = hn.shape
    L = S5_CHUNK
    m = tokens // L
    rows = min(S5_ROWS, m)
    kk = L * SSM_GROUP
    n_pair = LANES // (2 * SSM_GROUP)
    n_tiles = d // LANES
    kern = functools.partial(_s5_core_kernel, rows=rows, tiles_per_seq=chunks_per_seq // SUBLANES)
    per_tile = lambda *shape: pl.BlockSpec(shape, lambda lt, rb: (layer * n_tiles + lt,) + (0,) * (len(shape) - 1))
    return pl.pallas_call(
        kern,
        grid=(n_tiles, m // rows),
        in_specs=[
            pl.BlockSpec((rows * L, LANES), lambda lt, rb: (rb, lt)),
            per_tile(2 * n_pair, SSM_GROUP, kk),
            per_tile(n_pair, 2 * kk, kk),
            per_tile(n_pair, kk, 2 * kk),
            per_tile(n_pair, 2, 16, LANES),
        ],
        out_specs=pl.BlockSpec((rows * L, LANES), lambda lt, rb: (rb, lt)),
        out_shape=jax.ShapeDtypeStruct((tokens, d), F32),
        scratch_shapes=[
            pltpu.VMEM((2 * n_pair, kk, kk), BF16),
            pltpu.VMEM((n_pair, rows, 2 * kk), BF16),
            pltpu.VMEM((n_pair, rows, 2 * LANES), F32),
            pltpu.VMEM((n_pair, rows, 2 * LANES), F32),
            pltpu.VMEM((2 * n_pair, rows, kk), F32),
            pltpu.VMEM((2 * n_pair, SUBLANES, LANES), F32),
        ],
        compiler_params=pltpu.CompilerParams(
            dimension_semantics=("arbitrary", "arbitrary"), vmem_limit_bytes=VMEM_LIMIT_BYTES),
        name="s5_core",
    )(hn, kc, pb, qb, sc)


def _prenorm_kernel(x_ref, g_ref, o_ref):
    o_ref[...] = _rms(x_ref[...], g_ref[...]).astype(o_ref.dtype)


def _prenorm(x, g):
    t, d = x.shape
    tb = min(t, 1024)
    return pl.pallas_call(
        _prenorm_kernel,
        grid=(t // tb,),
        in_specs=[pl.BlockSpec((tb, d), lambda i: (i, 0)), pl.BlockSpec((1, d), lambda i: (0, 0))],
        out_specs=pl.BlockSpec((tb, d), lambda i: (i, 0)),
        out_shape=jax.ShapeDtypeStruct((t, d), F32),
        compiler_params=pltpu.CompilerParams(dimension_semantics=("arbitrary",)),
        name="prenorm",
    )(x, g)


def _rope(x, cos_t, sin_t, l64):
    half = ROT_DIM // 2
    lo = pltpu.roll(x, half, axis=1)
    hi = pltpu.roll(x, LANES - half, axis=1)
    return x * cos_t + jnp.where(l64 < half, -hi, lo) * sin_t


def _rope_tables(inv_ref, row0, nrows, seq):
    pos = (row0 + lax.broadcasted_iota(jnp.int32, (nrows, LANES), 0)) % seq
    ang = pos.astype(F32) * inv_ref[...]
    l64 = lax.broadcasted_iota(jnp.int32, (nrows, LANES), 1) % HEAD_DIM
    return jnp.cos(ang), jnp.sin(ang), l64


def _attention_consts():
    nq = ATTN_BLOCK
    ii = lax.broadcasted_iota(jnp.int32, (nq, 2 * nq), 0)
    jj = lax.broadcasted_iota(jnp.int32, (nq, 2 * nq), 1) % nq
    cur = jj <= ii
    lane = lax.broadcasted_iota(jnp.int32, (nq, LANES), 1)
    first = lane < HEAD_DIM
    return dict(ii=ii, jj=jj, cur=cur, first=first, zero=jnp.zeros((nq, LANES), BF16),
                cur_bf=jnp.where(cur, 1.0, 0.0).astype(BF16), prev_bf=jnp.where(cur, 0.0, 1.0).astype(BF16),
                ones_a=jnp.where(first, 1.0, 0.0).astype(BF16), ones_b=jnp.where(first, 0.0, 1.0).astype(BF16))


def _kv_blocks(k, v, c):
    first, zero = c["first"], c["zero"]
    k_blk = jnp.concatenate([jnp.where(first, k, zero), jnp.where(first, zero, k)], axis=0)
    v_blk = jnp.concatenate([
        jnp.concatenate([jnp.where(first, v, zero), c["ones_a"]], axis=1),
        jnp.concatenate([jnp.where(first, zero, v), c["ones_b"]], axis=1)], axis=0)
    return k_blk, v_blk


def _attention(q_ref, kc_ref, kp_ref, vc_ref, vp_ref, sink_ref, o_scr, c, *, row0, nrows, tb, seq):
    nq = ATTN_BLOCK
    pairs = GQA_GROUP // 2
    nt = (((1,), (1,)), ((), ()))
    blocks = {}

    def kv(kh, b):
        if (kh, b) not in blocks:
            if b < 0:
                blocks[kh, b] = _kv_blocks(kp_ref[kh], vp_ref[kh], c)
            else:
                rows = slice(b * nq, (b + 1) * nq)
                blocks[kh, b] = _kv_blocks(kc_ref[kh, rows, :], vc_ref[kh, rows, :], c)
        return blocks[kh, b]

    for b in range(row0 // nq, (row0 + nrows) // nq):
        rows = slice(b * nq, (b + 1) * nq)
        if b == 0:
            has_prev = (pl.program_id(0) % (seq // tb)) != 0
            valid = c["jj"] <= c["ii"] + jnp.where(has_prev, nq, 0)
        for kh in range(N_KV_HEADS):
            k_cur, v_cur = kv(kh, b)
            k_prev, v_prev = kv(kh, b - 1)
            for pair in range(pairs):
                hp = kh * pairs + pair
                qp = q_ref[hp, rows, :]
                s_cur = lax.dot_general(qp, k_cur, nt, preferred_element_type=F32)
                s_prev = lax.dot_general(qp, k_prev, nt, preferred_element_type=F32)
                s = jnp.where(c["cur"], s_cur, s_prev)
                if b == 0:
                    s = jnp.where(valid, s, NEG_INF)
                es, sinks = [], []
                for hh in range(2):
                    sh = s[:, hh * nq:(hh + 1) * nq]
                    sink = sink_ref[2 * hp + hh]
                    mx = jnp.maximum(jnp.max(sh, axis=-1, keepdims=True), sink)
                    es.append(jnp.exp(sh - mx))
                    sinks.append(jnp.exp(sink - mx))
                e = jnp.concatenate(es, axis=1).astype(BF16)
                o = _dot(e * c["cur_bf"], v_cur) + _dot(e * c["prev_bf"], v_prev)
                den = o[:, LANES:] + jnp.where(c["first"], sinks[0], sinks[1])
                o_scr[rows, hp * LANES:(hp + 1) * LANES] = (o[:, :LANES] / den).astype(o_scr.dtype)


def _layer_kernel(*refs, mixer, epilogue, tb, seq):
    it = iter(refs)
    h_ref = next(it)
    p_ref = next(it)
    if mixer == "s5":
        y_ref, gmix_ref, d_ref, wglu_ref = next(it), next(it), next(it), next(it)
    else:
        q_ref, kc_ref, kp_ref, vc_ref, vp_ref, sink_ref, wo_ref = (next(it) for _ in range(7))
    wup_ref, wdown_ref, wgate_ref, wproj_ref = (next(it) for _ in range(4))
    if epilogue in ("norm", "final"):
        gnext_ref = next(it)
    else:
        wq_ref, inv_ref = next(it), next(it)
        if epilogue == "kvq":
            wk_ref, wv_ref = next(it), next(it)
    ho_ref = None if epilogue == "final" else next(it)
    if epilogue == "norm":
        hn_ref = next(it)
    elif epilogue == "q":
        qo_ref = next(it)
    elif epilogue == "kvq":
        qo_ref, ko_ref, vo_ref = next(it), next(it), next(it)
    elif epilogue == "final":
        out_ref = next(it)
    if mixer == "attn":
        o_scr = next(it)

    d_model = h_ref.shape[-1]
    d_ff = wup_ref.shape[1]
    sub = min(SUB_BLOCK, tb)
    consts = _attention_consts() if mixer == "attn" else None

    def chain(r0):
        rows = slice(r0, r0 + sub)
        h = h_ref[rows, :]

        if mixer == "s5":
            u = _rms(h, gmix_ref[...])
            ge = _gelu_tanh(y_ref[rows, :] + d_ref[...] * u).astype(BF16)
            ab = _dot(ge, wglu_ref[...])
            mix = [ab[:, (2 * k) * GLU_TILE:(2 * k + 1) * GLU_TILE]
                   * jax.nn.sigmoid(ab[:, (2 * k + 1) * GLU_TILE:(2 * k + 2) * GLU_TILE])
                   for k in range(d_model // GLU_TILE)]
            h = h + jnp.concatenate(mix, axis=1)
        else:
            _attention(q_ref, kc_ref, kp_ref, vc_ref, vp_ref, sink_ref, o_scr, consts,
                       row0=r0, nrows=sub, tb=tb, seq=seq)
            h = h + _dot(o_scr[rows, :], wo_ref[...])

        s = _inv_rms(h)
        hb = h.astype(BF16)
        acc = None
        for c0 in range(0, d_ff, FF_CHUNK):
            up = _dot(hb, wup_ref[:, c0:c0 + FF_CHUNK])
            act = jnp.square(jnp.maximum(up, 0.0)).astype(BF16)
            part = _dot(act, wdown_ref[c0:c0 + FF_CHUNK, :])
            acc = part if acc is None else acc + part
        h = h + (s * s) * acc

        gate = _inv_rms(h) * _dot(h.astype(BF16), wgate_ref[...])
        proj = _dot(p_ref[rows, :].astype(BF16), wproj_ref[...])
        h = h + jax.nn.sigmoid(gate) * proj

        if epilogue == "final":
            out_ref[rows, :] = _rms(h, gnext_ref[...])
            return
        ho_ref[rows, :] = h
        if epilogue == "norm":
            hn_ref[rows, :] = _rms(h, gnext_ref[...]).astype(hn_ref.dtype)
            return
        cos_t, sin_t, l64 = _rope_tables(inv_ref, pl.program_id(0) * tb + r0, sub, seq)
        hb = h.astype(BF16)
        s = _inv_rms(h)
        q = _dot(hb, wq_ref[...])
        if epilogue == "kvq":
            k = _dot(hb, wk_ref[...])
            v = _dot(hb, wv_ref[...])
        qs = s * HEAD_DIM ** -0.5
        for hp in range(d_model // LANES):
            tile = _rope(q[:, hp * LANES:(hp + 1) * LANES] * qs, cos_t, sin_t, l64)
            qo_ref[hp, rows, :] = tile.astype(qo_ref.dtype)
        if epilogue == "kvq":
            for kh in range(N_KV_HEADS):
                tile = _rope(k[:, kh * LANES:(kh + 1) * LANES] * s, cos_t, sin_t, l64)
                ko_ref[kh, rows, :] = tile.astype(ko_ref.dtype)
                vo_ref[kh, rows, :] = (v[:, kh * LANES:(kh + 1) * LANES] * s).astype(vo_ref.dtype)

    for r0 in range(0, tb, sub):
        chain(r0)


def _weight(w):
    if not isinstance(w, tuple):
        return w, _resident(w.shape)
    arr, idx = w
    nd = arr.ndim - 1
    return arr, pl.BlockSpec((None,) + arr.shape[1:], lambda i: (idx,) + (0,) * nd, pipeline_mode=pl.Buffered(1))


def _layer_call(h, p, mixer_args, tail_args, epi_args, *, mixer, epilogue, seq):
    t, d = h.shape
    tb = min(TOKEN_BLOCK, seq)
    assert seq % tb == 0 and tb % ATTN_BLOCK == 0
    nsteps = t // tb
    row = lambda w: pl.BlockSpec((tb, w), lambda i: (i, 0))
    vec = lambda: pl.BlockSpec((1, d), lambda i: (0, 0))
    heads = lambda n: pl.BlockSpec((n, tb, LANES), lambda i: (0, i, 0))

    args, specs = [], []

    def add(arr, spec):
        args.append(arr)
        specs.append(spec)

    def add_weight(w):
        add(*_weight(w))

    p_all, layer = p
    add(h, row(d))
    add(p_all, pl.BlockSpec((None, tb, p_all.shape[-1]), lambda i: (layer, i, 0)))
    if mixer == "s5":
        y, gmix, dskip, wglu = mixer_args
        add(y, row(d))
        add(gmix, vec())
        add(dskip, vec())
        add_weight(wglu)
    else:
        q, kd, vd, sinks, wo = mixer_args
        blocks_per_tb = tb // ATTN_BLOCK
        prev = lambda: pl.BlockSpec((N_KV_HEADS, ATTN_BLOCK, LANES),
                                    lambda i: (0, jnp.maximum(i * blocks_per_tb - 1, 0), 0))
        add(q, heads(d // LANES))
        add(kd, heads(N_KV_HEADS))
        add(kd, prev())
        add(vd, heads(N_KV_HEADS))
        add(vd, prev())
        add(sinks, pl.BlockSpec(memory_space=pltpu.SMEM))
        add_weight(wo)
    for w in tail_args:
        add_weight(w)

    out_shapes, out_specs = [], []
    if epilogue != "final":
        out_shapes.append(jax.ShapeDtypeStruct((t, d), F32))
        out_specs.append(row(d))
    if epilogue in ("norm", "final"):
        add(epi_args[0], vec())
        out_shapes.append(jax.ShapeDtypeStruct((t, d), F32))
        out_specs.append(row(d))
    else:
        add_weight(epi_args[0])
        add(epi_args[1], pl.BlockSpec((1, LANES), lambda i: (0, 0)))
        out_shapes.append(jax.ShapeDtypeStruct((d // LANES, t, LANES), BF16))
        out_specs.append(heads(d // LANES))
        if epilogue == "kvq":
            add_weight(epi_args[2])
            add_weight(epi_args[3])
            out_shapes += [jax.ShapeDtypeStruct((N_KV_HEADS, t, LANES), BF16)] * 2
            out_specs += [heads(N_KV_HEADS), heads(N_KV_HEADS)]

    scratch = [pltpu.VMEM((tb, d), BF16)] if mixer == "attn" else []
    kern = functools.partial(_layer_kernel, mixer=mixer, epilogue=epilogue, tb=tb, seq=seq)
    return pl.pallas_call(
        kern,
        grid=(nsteps,),
        in_specs=specs,
        out_specs=out_specs,
        out_shape=out_shapes,
        scratch_shapes=scratch,
        compiler_params=pltpu.CompilerParams(
            dimension_semantics=("arbitrary",), vmem_limit_bytes=VMEM_LIMIT_BYTES),
        name=f"layer_{mixer}_{epilogue}",
    )(*args)


def kernel(x, p, norm_mix, ssm_lambda_re, ssm_lambda_im, ssm_log_dt, ssm_b_re, ssm_b_im, ssm_c_re, ssm_c_im, ssm_d, ssm_w_glu, kv_norm, w_k, w_v, w_q, attn_sinks, w_o, norm_mlp, w_up, w_down, norm_ple, w_ple_gate, w_ple_proj, norm_final):
    bsz, seq, d = x.shape
    depth = p.shape[0]
    n_a = ssm_lambda_re.shape[0]
    t = bsz * seq
    L = S5_CHUNK
    assert seq % (L * SUBLANES) == 0 and seq % ATTN_BLOCK == 0 and d % LANES == 0

    vec = lambda g: g.reshape(1, d)
    inv = ROPE_THETA ** (-jnp.arange(0, ROT_DIM, 2, dtype=F32) / ROT_DIM)
    inv64 = jnp.concatenate([inv, inv, jnp.zeros((HEAD_DIM - ROT_DIM,), F32)])
    inv_lanes = jnp.tile(inv64, LANES // HEAD_DIM).reshape(1, LANES)
    dup = lambda w: jnp.repeat(w.reshape(d, N_KV_HEADS, 1, HEAD_DIM), 2, axis=2).reshape(d, 2 * N_KV_HEADS * HEAD_DIM)

    p_all = p.reshape(depth, t, p.shape[-1])
    n_glu = d // GLU_TILE
    glu_bf = jnp.transpose(ssm_w_glu.reshape(n_a, d, 2, n_glu, GLU_TILE), (0, 1, 3, 2, 4)).reshape(n_a, d, 2 * d).astype(BF16)
    up_bf = (norm_mlp[:, :, None] * w_up).astype(BF16)
    down_bf = w_down.astype(BF16)
    gate_bf = (norm_ple[:, :, None] * w_ple_gate).astype(BF16)
    proj_bf = w_ple_proj.astype(BF16)
    q_bf = (norm_mix[n_a:, :, None] * w_q).astype(BF16)
    o_bf = w_o.astype(BF16)
    k_bf = dup(kv_norm[:, None] * w_k).astype(BF16)
    v_bf = dup(kv_norm[:, None] * w_v).astype(BF16)

    merge = lambda a: a.reshape((-1,) + a.shape[2:])
    tables = _s5_tables(merge(ssm_lambda_re), merge(ssm_lambda_im), merge(ssm_log_dt), merge(ssm_b_re),
                        merge(ssm_b_im), merge(ssm_c_re), merge(ssm_c_im))

    h = x.reshape(t, d)
    hn = _prenorm(h, vec(norm_mix[0]))
    q = kd = vd = None
    out = None
    for i in range(depth):
        tail = ((up_bf, i), (down_bf, i), (gate_bf, i), (proj_bf, i))
        if i == depth - 1:
            epilogue, epi = "final", (vec(norm_final),)
        elif i + 1 < n_a:
            epilogue, epi = "norm", (vec(norm_mix[i + 1]),)
        elif i + 1 == n_a:
            epilogue, epi = "kvq", ((q_bf, 0), inv_lanes, k_bf, v_bf)
        else:
            epilogue, epi = "q", ((q_bf, i + 1 - n_a), inv_lanes)

        if i < n_a:
            y = _s5_core(hn, *tables, layer=i, chunks_per_seq=seq // L)
            mixer, mix = "s5", (y, vec(norm_mix[i]), vec(ssm_d[i]), (glu_bf, i))
        else:
            j = i - n_a
            mixer, mix = "attn", (q, kd, vd, attn_sinks[j], (o_bf, j))

        res = _layer_call(h, (p_all, i), mix, tail, epi, mixer=mixer, epilogue=epilogue, seq=seq)
        if epilogue == "final":
            out = res[0]
        elif epilogue == "norm":
            h, hn = res
        elif epilogue == "q":
            h, q = res
        else:
            h, q, kd, vd = res
    return out.reshape(bsz, seq, d)
```

```python
import functools
import math

import jax
import jax.numpy as jnp
from jax import lax
from jax.experimental import pallas as pl
from jax.experimental.pallas import tpu as pltpu

SSM_GROUP = 16
SSM_STATE = 64
HEAD_DIM = 64
N_KV_HEADS = 4
GQA_GROUP = 4
ATTN_BLOCK = 128
WINDOW = 128
ROPE_THETA = 500000.0
ROT_DIM = 16
RMS_EPS = 1e-6
NEG_INF = -1e30

LANES = 128
SUBLANES = 8
VMEM_LIMIT_BYTES = 56 * 1024 * 1024

S5_CHUNK = 16
S5_ROWS = 512
SC_ROWS = 5 * SUBLANES
TOKEN_BLOCK = 512
SUB_BLOCK = 256
FF_CHUNK = 1024

BF16 = jnp.bfloat16
F32 = jnp.float32


def _dot(a, b):
    return jnp.dot(a, b, preferred_element_type=F32)


def _inv_rms(x):
    return lax.rsqrt(jnp.mean(x * x, axis=-1, keepdims=True) + RMS_EPS)


def _rms(x, g):
    return x * _inv_rms(x) * g


def _gelu_tanh(x):
    c = math.sqrt(2.0 / math.pi)
    return 0.5 * x * (1.0 + jnp.tanh(c * (x + 0.044715 * (x * x * x))))


def _resident(shape):
    nd = len(shape)
    return pl.BlockSpec(shape, lambda i: (0,) * nd, pipeline_mode=pl.Buffered(1))


def _s5_tables(lam_re, lam_im, log_dt, b_re, b_im, c_re, c_im):
    hp = lax.Precision.HIGHEST
    L = S5_CHUNK
    G, N = lam_re.shape
    H = b_re.shape[-1]
    dt = jnp.exp(log_dt)[:, None]
    lr, ph = lam_re * dt, lam_im * dt
    d = jnp.arange(L + 1, dtype=F32)[:, None, None]
    pw_r = jnp.exp(d * lr) * jnp.cos(d * ph)
    pw_i = jnp.exp(d * lr) * jnp.sin(d * ph)
    a_r, a_i = pw_r[1], pw_i[1]
    den = lam_re * lam_re + lam_im * lam_im
    nr = a_r - 1.0
    coef_r = (nr * lam_re + a_i * lam_im) / den
    coef_i = (a_i * lam_re - nr * lam_im) / den
    bb_r = coef_r[..., None] * b_re - coef_i[..., None] * b_im
    bb_i = coef_r[..., None] * b_im + coef_i[..., None] * b_re
    ab_r = pw_r[:L, :, :, None] * bb_r - pw_i[:L, :, :, None] * bb_i
    ab_i = pw_r[:L, :, :, None] * bb_i + pw_i[:L, :, :, None] * bb_r
    kc = (jnp.einsum('dgnh,gkn->ghdk', ab_r, c_re, precision=hp)
          - jnp.einsum('dgnh,gkn->ghdk', ab_i, c_im, precision=hp)).reshape(G, H, L * H)

    p_r = jnp.transpose(ab_r[::-1], (1, 0, 3, 2)).reshape(G // 2, 2, L * H, N)
    p_i = jnp.transpose(ab_i[::-1], (1, 0, 3, 2)).reshape(G // 2, 2, L * H, N)
    zp = jnp.zeros_like(p_r[:, 0])
    pb = jnp.concatenate([jnp.concatenate([p_r[:, 0], zp, p_i[:, 0], zp], axis=-1),
                          jnp.concatenate([zp, p_r[:, 1], zp, p_i[:, 1]], axis=-1)], axis=1)

    ca_r = c_re[None] * pw_r[1:, :, None, :] - c_im[None] * pw_i[1:, :, None, :]
    ca_i = c_re[None] * pw_i[1:, :, None, :] + c_im[None] * pw_r[1:, :, None, :]
    q_r = jnp.transpose(ca_r, (1, 3, 0, 2)).reshape(G // 2, 2, N, L * H)
    q_i = jnp.transpose(-ca_i, (1, 3, 0, 2)).reshape(G // 2, 2, N, L * H)
    zq = jnp.zeros_like(q_r[:, 0])
    qb = jnp.concatenate([jnp.concatenate([q_r[:, 0], zq], axis=-1), jnp.concatenate([zq, q_r[:, 1]], axis=-1),
                          jnp.concatenate([q_i[:, 0], zq], axis=-1), jnp.concatenate([zq, q_i[:, 1]], axis=-1)],
                         axis=1)

    k8 = list(range(SUBLANES))
    powers = [k + 1 for k in k8] + [SUBLANES] * SUBLANES
    keep = [1.0] * (2 * SUBLANES)
    for s in (1, 2, 4):
        powers += [s] * SUBLANES
        keep += [float(k >= s) for k in k8]
    mm = jnp.array(powers, F32)[:, None, None] * L
    keep = jnp.array(keep, F32)[:, None, None]
    sc = jnp.stack([keep * jnp.exp(mm * lr) * jnp.cos(mm * ph), keep * jnp.exp(mm * lr) * jnp.sin(mm * ph)])
    sc = jnp.transpose(sc.reshape(2, SC_ROWS, G // 2, 2 * N), (2, 0, 1, 3))
    return kc, pb.astype(BF16), qb.astype(BF16), sc


def _block_transpose(vs, lane_blk):
    vs = list(vs)
    for d in (4, 2, 1):
        take = (lane_blk & d) != 0
        for r in range(len(vs)):
            if r & d:
                continue
            top, bot = vs[r], vs[r + d]
            if 2 * d * SSM_GROUP == LANES:
                swapped = pltpu.roll(jnp.where(take, top, bot), d * SSM_GROUP, axis=1)
                vs[r] = jnp.where(take, swapped, top)
                vs[r + d] = jnp.where(take, bot, swapped)
            else:
                vs[r] = jnp.where(take, pltpu.roll(bot, d * SSM_GROUP, axis=1), top)
                vs[r + d] = jnp.where(take, bot, pltpu.roll(top, LANES - d * SSM_GROUP, axis=1))
    return vs


def _toeplitz_rows(kc, j, lane):
    lo, hi = kc[:, :LANES], kc[:, LANES:]
    r = (j % SUBLANES) * SSM_GROUP
    if r:
        lo_s, hi_s = pltpu.roll(lo, r, axis=1), pltpu.roll(hi, r, axis=1)
        first = jnp.where(lane < r, 0.0, lo_s)
        second = jnp.where(lane < r, lo_s, hi_s)
    else:
        first, second = lo, hi
    if j < SUBLANES:
        return jnp.concatenate([first, second], axis=1)
    return jnp.concatenate([jnp.zeros_like(first), first], axis=1)


def _s5_core_kernel(*refs, rows, tiles_per_seq):
    L = S5_CHUNK
    hn_ref, kc_ref, p_ref, q_ref, sc_ref, y_ref, t_scr, u_scr, v_scr, x_scr, yg_scr, carry_scr = refs
    n_pair = p_ref.shape[0]
    kk = t_scr.shape[1]
    rb = pl.program_id(1)
    lane_blk = lax.broadcasted_iota(jnp.int32, (rows, LANES), 1) // SSM_GROUP

    @pl.when(rb == 0)
    def _():
        carry_scr[...] = jnp.zeros_like(carry_scr)
        lane = lax.broadcasted_iota(jnp.int32, (SSM_GROUP, LANES), 1)
        for g in range(2 * n_pair):
            kc = kc_ref[g]
            for j in range(L):
                t_scr[g, j * SSM_GROUP:(j + 1) * SSM_GROUP, :] = _toeplitz_rows(kc, j, lane).astype(BF16)

    for jh in range(L // SUBLANES):
        vs = [hn_ref[pl.ds(jh * SUBLANES + r, rows, stride=L), :].astype(BF16) for r in range(SUBLANES)]
        vs = _block_transpose(vs, lane_blk)
        for gl in range(2 * n_pair):
            col = ((gl % 2) * (L // SUBLANES) + jh) * LANES
            u_scr[gl // 2, :, col:col + LANES] = vs[gl]

    for gp in range(n_pair):
        v_scr[gp] = _dot(u_scr[gp], p_ref[gp])

    half = LANES
    row = lax.broadcasted_iota(jnp.int32, (SUBLANES, half), 0)
    tiles = rows // SUBLANES

    def tile_body(t, carry):
        keep = ((rb * tiles + t) % tiles_per_seq) != 0
        r0 = pl.multiple_of(t * SUBLANES, SUBLANES)
        out = []
        for gp in range(n_pair):
            cr = jnp.where(keep, carry[2 * gp], 0.0)
            ci = jnp.where(keep, carry[2 * gp + 1], 0.0)
            r = v_scr[gp, pl.ds(r0, SUBLANES), 0:half]
            i = v_scr[gp, pl.ds(r0, SUBLANES), half:2 * half]
            for s_idx, s in enumerate((1, 2, 4)):
                lo = (2 + s_idx) * SUBLANES
                ar = sc_ref[gp, 0, lo:lo + SUBLANES, :]
                ai = sc_ref[gp, 1, lo:lo + SUBLANES, :]
                rs = pltpu.roll(r, s, axis=0)
                is_ = pltpu.roll(i, s, axis=0)
                r, i = r + (ar * rs - ai * is_), i + (ar * is_ + ai * rs)
            pw_r = sc_ref[gp, 0, 0:SUBLANES, :]
            pw_i = sc_ref[gp, 1, 0:SUBLANES, :]
            inc_r = r + (pw_r * cr - pw_i * ci)
            inc_i = i + (pw_r * ci + pw_i * cr)
            x_scr[gp, pl.ds(r0, SUBLANES), 0:half] = jnp.where(row == 0, cr, pltpu.roll(inc_r, 1, axis=0))
            x_scr[gp, pl.ds(r0, SUBLANES), half:2 * half] = jnp.where(row == 0, ci, pltpu.roll(inc_i, 1, axis=0))
            last_r = jnp.broadcast_to(r[SUBLANES - 1:SUBLANES, :], (SUBLANES, half))
            last_i = jnp.broadcast_to(i[SUBLANES - 1:SUBLANES, :], (SUBLANES, half))
            a8r = sc_ref[gp, 0, SUBLANES:2 * SUBLANES, :]
            a8i = sc_ref[gp, 1, SUBLANES:2 * SUBLANES, :]
            out += [last_r + (a8r * cr - a8i * ci), last_i + (a8r * ci + a8i * cr)]
        return tuple(out)

    init = tuple(carry_scr[k] for k in range(2 * n_pair))
    fin = lax.fori_loop(0, tiles, tile_body, init, unroll=2)
    for k in range(2 * n_pair):
        carry_scr[k] = fin[k]

    for gp in range(n_pair):
        carry_y = _dot(x_scr[gp].astype(BF16), q_ref[gp])
        for g2 in range(2):
            cols = slice(g2 * kk, (g2 + 1) * kk)
            yg_scr[2 * gp + g2] = _dot(u_scr[gp, :, cols], t_scr[2 * gp + g2]) + carry_y[:, cols]

    for ih in range(L // SUBLANES):
        vs = [yg_scr[gl, :, ih * LANES:(ih + 1) * LANES] for gl in range(2 * n_pair)]
        vs = _block_transpose(vs, lane_blk)
        for r in range(SUBLANES):
            y_ref[pl.ds(ih * SUBLANES + r, rows, stride=L), :] = vs[r]


def _s5_core(hn, kc, pb, qb, sc, *, layer, chunks_per_seq):
    tokens, d = hn.shape
    L = S5_CHUNK
    m = tokens // L
    rows = min(S5_ROWS, m)
    kk = L * SSM_GROUP
    n_pair = LANES // (2 * SSM_GROUP)
    n_tiles = d // LANES
    kern = functools.partial(_s5_core_kernel, rows=rows, tiles_per_seq=chunks_per_seq // SUBLANES)
    per_tile = lambda *shape: pl.BlockSpec(shape, lambda lt, rb: (layer * n_tiles + lt,) + (0,) * (len(shape) - 1))
    return pl.pallas_call(
        kern,
        grid=(n_tiles, m // rows),
        in_specs=[
            pl.BlockSpec((rows * L, LANES), lambda lt, rb: (rb, lt)),
            per_tile(2 * n_pair, SSM_GROUP, kk),
            per_tile(n_pair, 2 * kk, kk),
            per_tile(n_pair, kk, 2 * kk),
            per_tile(n_pair, 2, SC_ROWS, LANES),
        ],
        out_specs=pl.BlockSpec((rows * L, LANES), lambda lt, rb: (rb, lt)),
        out_shape=jax.ShapeDtypeStruct((tokens, d), F32),
        scratch_shapes=[
            pltpu.VMEM((2 * n_pair, kk, kk), BF16),
            pltpu.VMEM((n_pair, rows, 2 * kk), BF16),
            pltpu.VMEM((n_pair, rows, 2 * LANES), F32),
            pltpu.VMEM((n_pair, rows, 2 * LANES), F32),
            pltpu.VMEM((2 * n_pair, rows, kk), F32),
            pltpu.VMEM((2 * n_pair, SUBLANES, LANES), F32),
        ],
        compiler_params=pltpu.CompilerParams(
            dimension_semantics=("arbitrary", "arbitrary"), vmem_limit_bytes=VMEM_LIMIT_BYTES),
        name="s5_core",
    )(hn, kc, pb, qb, sc)


def _prenorm_kernel(x_ref, g_ref, o_ref):
    o_ref[...] = _rms(x_ref[...], g_ref[...]).astype(o_ref.dtype)


def _prenorm(x, g):
    t, d = x.shape
    tb = min(t, 1024)
    return pl.pallas_call(
        _prenorm_kernel,
        grid=(t // tb,),
        in_specs=[pl.BlockSpec((tb, d), lambda i: (i, 0)), pl.BlockSpec((1, d), lambda i: (0, 0))],
        out_specs=pl.BlockSpec((tb, d), lambda i: (i, 0)),
        out_shape=jax.ShapeDtypeStruct((t, d), F32),
        compiler_params=pltpu.CompilerParams(dimension_semantics=("arbitrary",)),
        name="prenorm",
    )(x, g)


def _rope(x, cos_t, sin_t, l64):
    half = ROT_DIM // 2
    lo = pltpu.roll(x, half, axis=1)
    hi = pltpu.roll(x, LANES - half, axis=1)
    return x * cos_t + jnp.where(l64 < half, -hi, lo) * sin_t


def _rope_tables(inv_ref, row0, nrows, seq):
    pos = (row0 + lax.broadcasted_iota(jnp.int32, (nrows, LANES), 0)) % seq
    ang = pos.astype(F32) * inv_ref[...]
    l64 = lax.broadcasted_iota(jnp.int32, (nrows, LANES), 1) % HEAD_DIM
    return jnp.cos(ang), jnp.sin(ang), l64


def _attention_consts():
    nq = ATTN_BLOCK
    ii = lax.broadcasted_iota(jnp.int32, (nq, 2 * nq), 0)
    jj = lax.broadcasted_iota(jnp.int32, (nq, 2 * nq), 1) % nq
    cur = jj <= ii
    lane = lax.broadcasted_iota(jnp.int32, (nq, LANES), 1)
    first = lane < HEAD_DIM
    return dict(ii=ii, jj=jj, cur=cur, first=first, zero=jnp.zeros((nq, LANES), BF16),
                cur_bf=jnp.where(cur, 1.0, 0.0).astype(BF16), prev_bf=jnp.where(cur, 0.0, 1.0).astype(BF16),
                ones_a=jnp.where(first, 1.0, 0.0).astype(BF16), ones_b=jnp.where(first, 0.0, 1.0).astype(BF16))


def _kv_blocks(k, v, c):
    first, zero = c["first"], c["zero"]
    k_blk = jnp.concatenate([jnp.where(first, k, zero), jnp.where(first, zero, k)], axis=0)
    v_blk = jnp.concatenate([
        jnp.concatenate([jnp.where(first, v, zero), c["ones_a"]], axis=1),
        jnp.concatenate([jnp.where(first, zero, v), c["ones_b"]], axis=1)], axis=0)
    return k_blk, v_blk


def _attention(q_ref, kc_ref, kp_ref, vc_ref, vp_ref, sink_ref, o_scr, c, *, row0, nrows, tb, seq):
    nq = ATTN_BLOCK
    pairs = GQA_GROUP // 2
    nt = (((1,), (1,)), ((), ()))
    blocks = {}

    def kv(kh, b):
        if (kh, b) not in blocks:
            if b < 0:
                blocks[kh, b] = _kv_blocks(kp_ref[kh], vp_ref[kh], c)
            else:
                rows = slice(b * nq, (b + 1) * nq)
                blocks[kh, b] = _kv_blocks(kc_ref[kh, rows, :], vc_ref[kh, rows, :], c)
        return blocks[kh, b]

    for b in range(row0 // nq, (row0 + nrows) // nq):
        rows = slice(b * nq, (b + 1) * nq)
        if b == 0:
            has_prev = (pl.program_id(0) % (seq // tb)) != 0
            valid = c["jj"] <= c["ii"] + jnp.where(has_prev, nq, 0)
        for kh in range(N_KV_HEADS):
            k_cur, v_cur = kv(kh, b)
            k_prev, v_prev = kv(kh, b - 1)
            for pair in range(pairs):
                hp = kh * pairs + pair
                qp = q_ref[hp, rows, :]
                s_cur = lax.dot_general(qp, k_cur, nt, preferred_element_type=F32)
                s_prev = lax.dot_general(qp, k_prev, nt, preferred_element_type=F32)
                s = jnp.where(c["cur"], s_cur, s_prev)
                if b == 0:
                    s = jnp.where(valid, s, NEG_INF)
                es, sinks = [], []
                for hh in range(2):
                    sh = s[:, hh * nq:(hh + 1) * nq]
                    sink = sink_ref[2 * hp + hh]
                    mx = jnp.maximum(jnp.max(sh, axis=-1, keepdims=True), sink)
                    es.append(jnp.exp(sh - mx))
                    sinks.append(jnp.exp(sink - mx))
                e = jnp.concatenate(es, axis=1).astype(BF16)
                o = _dot(e * c["cur_bf"], v_cur) + _dot(e * c["prev_bf"], v_prev)
                den = o[:, LANES:] + jnp.where(c["first"], sinks[0], sinks[1])
                o_scr[rows, hp * LANES:(hp + 1) * LANES] = (o[:, :LANES] / den).astype(o_scr.dtype)


def _layer_kernel(*refs, mixer, epilogue, tb, seq):
    it = iter(refs)
    h_ref = next(it)
    p_ref = next(it)
    if mixer == "s5":
        y_ref, gmix_ref, d_ref, wglu_ref = next(it), next(it), next(it), next(it)
    else:
        q_ref, kc_ref, kp_ref, vc_ref, vp_ref, sink_ref, wo_ref = (next(it) for _ in range(7))
    wup_ref, wdown_ref, wgate_ref, wproj_ref = (next(it) for _ in range(4))
    if epilogue in ("norm", "final"):
        gnext_ref = next(it)
    else:
        wq_ref, inv_ref = next(it), next(it)
        if epilogue == "kvq":
            wk_ref, wv_ref = next(it), next(it)
    ho_ref = None if epilogue == "final" else next(it)
    if epilogue == "norm":
        hn_ref = next(it)
    elif epilogue == "q":
        qo_ref = next(it)
    elif epilogue == "kvq":
        qo_ref, ko_ref, vo_ref = next(it), next(it), next(it)
    elif epilogue == "final":
        out_ref = next(it)
    if mixer == "attn":
        o_scr = next(it)

    d_model = h_ref.shape[-1]
    d_ff = wup_ref.shape[1]
    sub = min(SUB_BLOCK, tb)
    consts = _attention_consts() if mixer == "attn" else None

    def chain(r0):
        rows = slice(r0, r0 + sub)
        h = h_ref[rows, :]

        if mixer == "s5":
            u = _rms(h, gmix_ref[...])
            ge = _gelu_tanh(y_ref[rows, :] + d_ref[...] * u).astype(BF16)
            ab = _dot(ge, wglu_ref[...])
            h = h + ab[:, :d_model] * jax.nn.sigmoid(ab[:, d_model:])
        else:
            _attention(q_ref, kc_ref, kp_ref, vc_ref, vp_ref, sink_ref, o_scr, consts,
                       row0=r0, nrows=sub, tb=tb, seq=seq)
            h = h + _dot(o_scr[rows, :], wo_ref[...])

        s = _inv_rms(h)
        hb = h.astype(BF16)
        acc = None
        for c0 in range(0, d_ff, FF_CHUNK):
            up = _dot(hb, wup_ref[:, c0:c0 + FF_CHUNK])
            act = jnp.square(jnp.maximum(up, 0.0)).astype(BF16)
            part = _dot(act, wdown_ref[c0:c0 + FF_CHUNK, :])
            acc = part if acc is None else acc + part
        h = h + (s * s) * acc

        gate = _inv_rms(h) * _dot(h.astype(BF16), wgate_ref[...])
        proj = _dot(p_ref[rows, :].astype(BF16), wproj_ref[...])
        h = h + jax.nn.sigmoid(gate) * proj

        if epilogue == "final":
            out_ref[rows, :] = _rms(h, gnext_ref[...])
            return
        ho_ref[rows, :] = h
        if epilogue == "norm":
            hn_ref[rows, :] = _rms(h, gnext_ref[...]).astype(hn_ref.dtype)
            return
        cos_t, sin_t, l64 = _rope_tables(inv_ref, pl.program_id(0) * tb + r0, sub, seq)
        hb = h.astype(BF16)
        s = _inv_rms(h)
        q = _dot(hb, wq_ref[...])
        if epilogue == "kvq":
            k = _dot(hb, wk_ref[...])
            v = _dot(hb, wv_ref[...])
        qs = s * HEAD_DIM ** -0.5
        for hp in range(d_model // LANES):
            tile = _rope(q[:, hp * LANES:(hp + 1) * LANES] * qs, cos_t, sin_t, l64)
            qo_ref[hp, rows, :] = tile.astype(qo_ref.dtype)
        if epilogue == "kvq":
            for kh in range(N_KV_HEADS):
                tile = _rope(k[:, kh * LANES:(kh + 1) * LANES] * s, cos_t, sin_t, l64)
                ko_ref[kh, rows, :] = tile.astype(ko_ref.dtype)
                vo_ref[kh, rows, :] = (v[:, kh * LANES:(kh + 1) * LANES] * s).astype(vo_ref.dtype)

    for r0 in range(0, tb, sub):
        chain(r0)


def _weight(w):
    if not isinstance(w, tuple):
        return w, _resident(w.shape)
    arr, idx = w
    nd = arr.ndim - 1
    return arr, pl.BlockSpec((None,) + arr.shape[1:], lambda i: (idx,) + (0,) * nd, pipeline_mode=pl.Buffered(1))


def _layer_call(h, p, mixer_args, tail_args, epi_args, *, mixer, epilogue, seq):
    t, d = h.shape
    tb = min(TOKEN_BLOCK, seq)
    assert seq % tb == 0 and tb % ATTN_BLOCK == 0
    nsteps = t // tb
    row = lambda w: pl.BlockSpec((tb, w), lambda i: (i, 0))
    vec = lambda: pl.BlockSpec((1, d), lambda i: (0, 0))
    heads = lambda n: pl.BlockSpec((n, tb, LANES), lambda i: (0, i, 0))

    args, specs = [], []

    def add(arr, spec):
        args.append(arr)
        specs.append(spec)

    def add_weight(w):
        add(*_weight(w))

    p_all, layer = p
    add(h, row(d))
    add(p_all, pl.BlockSpec((None, tb, p_all.shape[-1]), lambda i: (layer, i, 0)))
    if mixer == "s5":
        y, gmix, dskip, wglu = mixer_args
        add(y, row(d))
        add(gmix, vec())
        add(dskip, vec())
        add_weight(wglu)
    else:
        q, kd, vd, sinks, wo = mixer_args
        blocks_per_tb = tb // ATTN_BLOCK
        prev = lambda: pl.BlockSpec((N_KV_HEADS, ATTN_BLOCK, LANES),
                                    lambda i: (0, jnp.maximum(i * blocks_per_tb - 1, 0), 0))
        add(q, heads(d // LANES))
        add(kd, heads(N_KV_HEADS))
        add(kd, prev())
        add(vd, heads(N_KV_HEADS))
        add(vd, prev())
        add(sinks, pl.BlockSpec(memory_space=pltpu.SMEM))
        add_weight(wo)
    for w in tail_args:
        add_weight(w)

    out_shapes, out_specs = [], []
    if epilogue != "final":
        out_shapes.append(jax.ShapeDtypeStruct((t, d), F32))
        out_specs.append(row(d))
    if epilogue in ("norm", "final"):
        add(epi_args[0], vec())
        out_shapes.append(jax.ShapeDtypeStruct((t, d), F32))
        out_specs.append(row(d))
    else:
        add_weight(epi_args[0])
        add(epi_args[1], pl.BlockSpec((1, LANES), lambda i: (0, 0)))
        out_shapes.append(jax.ShapeDtypeStruct((d // LANES, t, LANES), BF16))
        out_specs.append(heads(d // LANES))
        if epilogue == "kvq":
            add_weight(epi_args[2])
            add_weight(epi_args[3])
            out_shapes += [jax.ShapeDtypeStruct((N_KV_HEADS, t, LANES), BF16)] * 2
            out_specs += [heads(N_KV_HEADS), heads(N_KV_HEADS)]

    scratch = [pltpu.VMEM((tb, d), BF16)] if mixer == "attn" else []
    kern = functools.partial(_layer_kernel, mixer=mixer, epilogue=epilogue, tb=tb, seq=seq)
    return pl.pallas_call(
        kern,
        grid=(nsteps,),
        in_specs=specs,
        out_specs=out_specs,
        out_shape=out_shapes,
        scratch_shapes=scratch,
        compiler_params=pltpu.CompilerParams(
            dimension_semantics=("arbitrary",), vmem_limit_bytes=VMEM_LIMIT_BYTES),
        name=f"layer_{mixer}_{epilogue}",
    )(*args)


def kernel(x, p, norm_mix, ssm_lambda_re, ssm_lambda_im, ssm_log_dt, ssm_b_re, ssm_b_im, ssm_c_re, ssm_c_im, ssm_d, ssm_w_glu, kv_norm, w_k, w_v, w_q, attn_sinks, w_o, norm_mlp, w_up, w_down, norm_ple, w_ple_gate, w_ple_proj, norm_final):
    bsz, seq, d = x.shape
    depth = p.shape[0]
    n_a = ssm_lambda_re.shape[0]
    t = bsz * seq
    L = S5_CHUNK
    assert seq % (L * SUBLANES) == 0 and seq % ATTN_BLOCK == 0 and d % LANES == 0

    vec = lambda g: g.reshape(1, d)
    inv = ROPE_THETA ** (-jnp.arange(0, ROT_DIM, 2, dtype=F32) / ROT_DIM)
    inv64 = jnp.concatenate([inv, inv, jnp.zeros((HEAD_DIM - ROT_DIM,), F32)])
    inv_lanes = jnp.tile(inv64, LANES // HEAD_DIM).reshape(1, LANES)
    dup = lambda w: jnp.repeat(w.reshape(d, N_KV_HEADS, 1, HEAD_DIM), 2, axis=2).reshape(d, 2 * N_KV_HEADS * HEAD_DIM)

    p_all = p.reshape(depth, t, p.shape[-1])
    glu_bf = ssm_w_glu.astype(BF16)
    up_bf = (norm_mlp[:, :, None] * w_up).astype(BF16)
    down_bf = w_down.astype(BF16)
    gate_bf = (norm_ple[:, :, None] * w_ple_gate).astype(BF16)
    proj_bf = w_ple_proj.astype(BF16)
    q_bf = (norm_mix[n_a:, :, None] * w_q).astype(BF16)
    o_bf = w_o.astype(BF16)
    k_bf = dup(kv_norm[:, None] * w_k).astype(BF16)
    v_bf = dup(kv_norm[:, None] * w_v).astype(BF16)

    merge = lambda a: a.reshape((-1,) + a.shape[2:])
    tables = _s5_tables(merge(ssm_lambda_re), merge(ssm_lambda_im), merge(ssm_log_dt), merge(ssm_b_re),
                        merge(ssm_b_im), merge(ssm_c_re), merge(ssm_c_im))

    h = x.reshape(t, d)
    hn = _prenorm(h, vec(norm_mix[0]))
    q = kd = vd = None
    out = None
    for i in range(depth):
        tail = ((up_bf, i), (down_bf, i), (gate_bf, i), (proj_bf, i))
        if i == depth - 1:
            epilogue, epi = "final", (vec(norm_final),)
        elif i + 1 < n_a:
            epilogue, epi = "norm", (vec(norm_mix[i + 1]),)
        elif i + 1 == n_a:
            epilogue, epi = "kvq", ((q_bf, 0), inv_lanes, k_bf, v_bf)
        else:
            epilogue, epi = "q", ((q_bf, i + 1 - n_a), inv_lanes)

        if i < n_a:
            y = _s5_core(hn, *tables, layer=i, chunks_per_seq=seq // L)
            mixer, mix = "s5", (y, vec(norm_mix[i]), vec(ssm_d[i]), (glu_bf, i))
        else:
            j = i - n_a
            mixer, mix = "attn", (q, kd, vd, attn_sinks[j], (o_bf, j))

        res = _layer_call(h, (p_all, i), mix, tail, epi, mixer=mixer, epilogue=epilogue, seq=seq)
        if epilogue == "final":
            out = res[0]
        elif epilogue == "norm":
            h, hn = res
        elif epilogue == "q":
            h, q = res
        else:
            h, q, kd, vd = res
    return out.reshape(bsz, seq, d)
```

```python
import functools
import math

import jax
import jax.numpy as jnp
from jax import lax
from jax.experimental import pallas as pl
from jax.experimental.pallas import tpu as pltpu

SSM_GROUP = 16
SSM_STATE = 64
HEAD_DIM = 64
N_KV_HEADS = 4
GQA_GROUP = 4
ATTN_BLOCK = 128
WINDOW = 128
ROPE_THETA = 500000.0
ROT_DIM = 16
RMS_EPS = 1e-6
NEG_INF = -1e30

LANES = 128
SUBLANES = 8
VMEM_LIMIT_BYTES = 56 * 1024 * 1024

S5_CHUNK = 16
S5_ROWS = 512
SC_ROWS = 5 * SUBLANES
TOKEN_BLOCK = 512
SUB_BLOCK = 256
FF_CHUNK = 1024
GLU_TILE = 256

BF16 = jnp.bfloat16
F32 = jnp.float32


def _dot(a, b):
    return jnp.dot(a, b, preferred_element_type=F32)


def _inv_rms(x):
    return lax.rsqrt(jnp.mean(x * x, axis=-1, keepdims=True) + RMS_EPS)


def _rms(x, g):
    return x * _inv_rms(x) * g


def _gelu_tanh(x):
    c = math.sqrt(2.0 / math.pi)
    return 0.5 * x * (1.0 + jnp.tanh(c * (x + 0.044715 * (x * x * x))))


def _resident(shape):
    nd = len(shape)
    return pl.BlockSpec(shape, lambda i: (0,) * nd, pipeline_mode=pl.Buffered(1))


def _s5_tables(lam_re, lam_im, log_dt, b_re, b_im, c_re, c_im):
    hp = lax.Precision.HIGHEST
    L = S5_CHUNK
    G, N = lam_re.shape
    H = b_re.shape[-1]
    dt = jnp.exp(log_dt)[:, None]
    lr, ph = lam_re * dt, lam_im * dt
    d = jnp.arange(L + 1, dtype=F32)[:, None, None]
    pw_r = jnp.exp(d * lr) * jnp.cos(d * ph)
    pw_i = jnp.exp(d * lr) * jnp.sin(d * ph)
    a_r, a_i = pw_r[1], pw_i[1]
    den = lam_re * lam_re + lam_im * lam_im
    nr = a_r - 1.0
    coef_r = (nr * lam_re + a_i * lam_im) / den
    coef_i = (a_i * lam_re - nr * lam_im) / den
    bb_r = coef_r[..., None] * b_re - coef_i[..., None] * b_im
    bb_i = coef_r[..., None] * b_im + coef_i[..., None] * b_re
    ab_r = pw_r[:L, :, :, None] * bb_r - pw_i[:L, :, :, None] * bb_i
    ab_i = pw_r[:L, :, :, None] * bb_i + pw_i[:L, :, :, None] * bb_r
    kc = (jnp.einsum('dgnh,gkn->ghdk', ab_r, c_re, precision=hp)
          - jnp.einsum('dgnh,gkn->ghdk', ab_i, c_im, precision=hp)).reshape(G, H, L * H)

    p_r = jnp.transpose(ab_r[::-1], (1, 0, 3, 2)).reshape(G // 2, 2, L * H, N)
    p_i = jnp.transpose(ab_i[::-1], (1, 0, 3, 2)).reshape(G // 2, 2, L * H, N)
    zp = jnp.zeros_like(p_r[:, 0])
    pb = jnp.concatenate([jnp.concatenate([p_r[:, 0], zp, p_i[:, 0], zp], axis=-1),
                          jnp.concatenate([zp, p_r[:, 1], zp, p_i[:, 1]], axis=-1)], axis=1)

    ca_r = c_re[None] * pw_r[1:, :, None, :] - c_im[None] * pw_i[1:, :, None, :]
    ca_i = c_re[None] * pw_i[1:, :, None, :] + c_im[None] * pw_r[1:, :, None, :]
    q_r = jnp.transpose(ca_r, (1, 3, 0, 2)).reshape(G // 2, 2, N, L * H)
    q_i = jnp.transpose(-ca_i, (1, 3, 0, 2)).reshape(G // 2, 2, N, L * H)
    zq = jnp.zeros_like(q_r[:, 0])
    qb = jnp.concatenate([jnp.concatenate([q_r[:, 0], zq], axis=-1), jnp.concatenate([zq, q_r[:, 1]], axis=-1),
                          jnp.concatenate([q_i[:, 0], zq], axis=-1), jnp.concatenate([zq, q_i[:, 1]], axis=-1)],
                         axis=1)

    k8 = list(range(SUBLANES))
    powers = [k + 1 for k in k8] + [SUBLANES] * SUBLANES
    keep = [1.0] * (2 * SUBLANES)
    for s in (1, 2, 4):
        powers += [s] * SUBLANES
        keep += [float(k >= s) for k in k8]
    mm = jnp.array(powers, F32)[:, None, None] * L
    keep = jnp.array(keep, F32)[:, None, None]
    sc = jnp.stack([keep * jnp.exp(mm * lr) * jnp.cos(mm * ph), keep * jnp.exp(mm * lr) * jnp.sin(mm * ph)])
    sc = jnp.transpose(sc.reshape(2, SC_ROWS, G // 2, 2 * N), (2, 0, 1, 3))
    return kc, pb.astype(BF16), qb.astype(BF16), sc


def _block_transpose(vs, lane_blk):
    vs = list(vs)
    for d in (4, 2, 1):
        take = (lane_blk & d) != 0
        for r in range(len(vs)):
            if r & d:
                continue
            top, bot = vs[r], vs[r + d]
            if 2 * d * SSM_GROUP == LANES:
                swapped = pltpu.roll(jnp.where(take, top, bot), d * SSM_GROUP, axis=1)
                vs[r] = jnp.where(take, swapped, top)
                vs[r + d] = jnp.where(take, bot, swapped)
            else:
                vs[r] = jnp.where(take, pltpu.roll(bot, d * SSM_GROUP, axis=1), top)
                vs[r + d] = jnp.where(take, bot, pltpu.roll(top, LANES - d * SSM_GROUP, axis=1))
    return vs


def _toeplitz_rows(kc, j, lane):
    lo, hi = kc[:, :LANES], kc[:, LANES:]
    r = (j % SUBLANES) * SSM_GROUP
    if r:
        lo_s, hi_s = pltpu.roll(lo, r, axis=1), pltpu.roll(hi, r, axis=1)
        first = jnp.where(lane < r, 0.0, lo_s)
        second = jnp.where(lane < r, lo_s, hi_s)
    else:
        first, second = lo, hi
    if j < SUBLANES:
        return jnp.concatenate([first, second], axis=1)
    return jnp.concatenate([jnp.zeros_like(first), first], axis=1)


def _s5_core_kernel(*refs, rows, tiles_per_seq):
    L = S5_CHUNK
    hn_ref, kc_ref, p_ref, q_ref, sc_ref, y_ref, t_scr, u_scr, v_scr, x_scr, yg_scr, carry_scr = refs
    n_pair = p_ref.shape[0]
    kk = t_scr.shape[1]
    rb = pl.program_id(1)
    lane_blk = lax.broadcasted_iota(jnp.int32, (rows, LANES), 1) // SSM_GROUP

    @pl.when(rb == 0)
    def _():
        carry_scr[...] = jnp.zeros_like(carry_scr)
        lane = lax.broadcasted_iota(jnp.int32, (SSM_GROUP, LANES), 1)
        for g in range(2 * n_pair):
            kc = kc_ref[g]
            for j in range(L):
                t_scr[g, j * SSM_GROUP:(j + 1) * SSM_GROUP, :] = _toeplitz_rows(kc, j, lane).astype(BF16)

    for jh in range(L // SUBLANES):
        vs = [hn_ref[pl.ds(jh * SUBLANES + r, rows, stride=L), :].astype(BF16) for r in range(SUBLANES)]
        vs = _block_transpose(vs, lane_blk)
        for gl in range(2 * n_pair):
            col = ((gl % 2) * (L // SUBLANES) + jh) * LANES
            u_scr[gl // 2, :, col:col + LANES] = vs[gl]

    for gp in range(n_pair):
        v_scr[gp] = _dot(u_scr[gp], p_ref[gp])

    half = LANES
    row = lax.broadcasted_iota(jnp.int32, (SUBLANES, half), 0)
    tiles = rows // SUBLANES

    def tile_body(t, carry):
        keep = ((rb * tiles + t) % tiles_per_seq) != 0
        r0 = pl.multiple_of(t * SUBLANES, SUBLANES)
        out = []
        for gp in range(n_pair):
            cr = jnp.where(keep, carry[2 * gp], 0.0)
            ci = jnp.where(keep, carry[2 * gp + 1], 0.0)
            r = v_scr[gp, pl.ds(r0, SUBLANES), 0:half]
            i = v_scr[gp, pl.ds(r0, SUBLANES), half:2 * half]
            for s_idx, s in enumerate((1, 2, 4)):
                lo = (2 + s_idx) * SUBLANES
                ar = sc_ref[gp, 0, lo:lo + SUBLANES, :]
                ai = sc_ref[gp, 1, lo:lo + SUBLANES, :]
                rs = pltpu.roll(r, s, axis=0)
                is_ = pltpu.roll(i, s, axis=0)
                r, i = r + (ar * rs - ai * is_), i + (ar * is_ + ai * rs)
            pw_r = sc_ref[gp, 0, 0:SUBLANES, :]
            pw_i = sc_ref[gp, 1, 0:SUBLANES, :]
            inc_r = r + (pw_r * cr - pw_i * ci)
            inc_i = i + (pw_r * ci + pw_i * cr)
            x_scr[gp, pl.ds(r0, SUBLANES), 0:half] = jnp.where(row == 0, cr, pltpu.roll(inc_r, 1, axis=0))
            x_scr[gp, pl.ds(r0, SUBLANES), half:2 * half] = jnp.where(row == 0, ci, pltpu.roll(inc_i, 1, axis=0))
            last_r = jnp.broadcast_to(r[SUBLANES - 1:SUBLANES, :], (SUBLANES, half))
            last_i = jnp.broadcast_to(i[SUBLANES - 1:SUBLANES, :], (SUBLANES, half))
            a8r = sc_ref[gp, 0, SUBLANES:2 * SUBLANES, :]
            a8i = sc_ref[gp, 1, SUBLANES:2 * SUBLANES, :]
            out += [last_r + (a8r * cr - a8i * ci), last_i + (a8r * ci + a8i * cr)]
        return tuple(out)

    init = tuple(carry_scr[k] for k in range(2 * n_pair))
    fin = lax.fori_loop(0, tiles, tile_body, init, unroll=2)
    for k in range(2 * n_pair):
        carry_scr[k] = fin[k]

    for gp in range(n_pair):
        carry_y = _dot(x_scr[gp].astype(BF16), q_ref[gp])
        for g2 in range(2):
            cols = slice(g2 * kk, (g2 + 1) * kk)
            yg_scr[2 * gp + g2] = _dot(u_scr[gp, :, cols], t_scr[2 * gp + g2]) + carry_y[:, cols]

    for ih in range(L // SUBLANES):
        vs = [yg_scr[gl, :, ih * LANES:(ih + 1) * LANES] for gl in range(2 * n_pair)]
        vs = _block_transpose(vs, lane_blk)
        for r in range(SUBLANES):
            y_ref[pl.ds(ih * SUBLANES + r, rows, stride=L), :] = vs[r]


def _s5_core(hn, kc, pb, qb, sc, *, layer, chunks_per_seq):
    tokens, d = hn.shape
    L = S5_CHUNK
    m = tokens // L
    rows = min(S5_ROWS, m)
    kk = L * SSM_GROUP
    n_pair = LANES // (2 * SSM_GROUP)
    n_tiles = d // LANES
    kern = functools.partial(_s5_core_kernel, rows=rows, tiles_per_seq=chunks_per_seq // SUBLANES)
    per_tile = lambda *shape: pl.BlockSpec(shape, lambda lt, rb: (layer * n_tiles + lt,) + (0,) * (len(shape) - 1))
    return pl.pallas_call(
        kern,
        grid=(n_tiles, m // rows),
        in_specs=[
            pl.BlockSpec((rows * L, LANES), lambda lt, rb: (rb, lt)),
            per_tile(2 * n_pair, SSM_GROUP, kk),
            per_tile(n_pair, 2 * kk, kk),
            per_tile(n_pair, kk, 2 * kk),
            per_tile(n_pair, 2, SC_ROWS, LANES),
        ],
        out_specs=pl.BlockSpec((rows * L, LANES), lambda lt, rb: (rb, lt)),
        out_shape=jax.ShapeDtypeStruct((tokens, d), F32),
        scratch_shapes=[
            pltpu.VMEM((2 * n_pair, kk, kk), BF16),
            pltpu.VMEM((n_pair, rows, 2 * kk), BF16),
            pltpu.VMEM((n_pair, rows, 2 * LANES), F32),
            pltpu.VMEM((n_pair, rows, 2 * LANES), F32),
            pltpu.VMEM((2 * n_pair, rows, kk), F32),
            pltpu.VMEM((2 * n_pair, SUBLANES, LANES), F32),
        ],
        compiler_params=pltpu.CompilerParams(
            dimension_semantics=("arbitrary", "arbitrary"), vmem_limit_bytes=VMEM_LIMIT_BYTES),
        name="s5_core",
    )(hn, kc, pb, qb, sc)


def _prenorm_kernel(x_ref, g_ref, o_ref):
    o_ref[...] = _rms(x_ref[...], g_ref[...]).astype(o_ref.dtype)


def _prenorm(x, g):
    t, d = x.shape
    tb = min(t, 1024)
    return pl.pallas_call(
        _prenorm_kernel,
        grid=(t // tb,),
        in_specs=[pl.BlockSpec((tb, d), lambda i: (i, 0)), pl.BlockSpec((1, d), lambda i: (0, 0))],
        out_specs=pl.BlockSpec((tb, d), lambda i: (i, 0)),
        out_shape=jax.ShapeDtypeStruct((t, d), F32),
        compiler_params=pltpu.CompilerParams(dimension_semantics=("arbitrary",)),
        name="prenorm",
    )(x, g)


def _rope(x, cos_t, sin_t, l64):
    half = ROT_DIM // 2
    lo = pltpu.roll(x, half, axis=1)
    hi = pltpu.roll(x, LANES - half, axis=1)
    return x * cos_t + jnp.where(l64 < half, -hi, lo) * sin_t


def _rope_tables(inv_ref, row0, nrows, seq):
    pos = (row0 + lax.broadcasted_iota(jnp.int32, (nrows, LANES), 0)) % seq
    ang = pos.astype(F32) * inv_ref[...]
    l64 = lax.broadcasted_iota(jnp.int32, (nrows, LANES), 1) % HEAD_DIM
    return jnp.cos(ang), jnp.sin(ang), l64


def _attention_consts():
    nq = ATTN_BLOCK
    ii = lax.broadcasted_iota(jnp.int32, (nq, 2 * nq), 0)
    jj = lax.broadcasted_iota(jnp.int32, (nq, 2 * nq), 1) % nq
    cur = jj <= ii
    lane = lax.broadcasted_iota(jnp.int32, (nq, LANES), 1)
    first = lane < HEAD_DIM
    return dict(ii=ii, jj=jj, cur=cur, first=first, zero=jnp.zeros((nq, LANES), BF16),
                cur_bf=jnp.where(cur, 1.0, 0.0).astype(BF16), prev_bf=jnp.where(cur, 0.0, 1.0).astype(BF16),
                ones_a=jnp.where(first, 1.0, 0.0).astype(BF16), ones_b=jnp.where(first, 0.0, 1.0).astype(BF16))


def _kv_blocks(k, v, c):
    first, zero = c["first"], c["zero"]
    k_blk = jnp.concatenate([jnp.where(first, k, zero), jnp.where(first, zero, k)], axis=0)
    v_blk = jnp.concatenate([
        jnp.concatenate([jnp.where(first, v, zero), c["ones_a"]], axis=1),
        jnp.concatenate([jnp.where(first, zero, v), c["ones_b"]], axis=1)], axis=0)
    return k_blk, v_blk


def _attention(q_ref, kc_ref, kp_ref, vc_ref, vp_ref, sink_ref, o_scr, c, *, row0, nrows, tb, seq):
    nq = ATTN_BLOCK
    pairs = GQA_GROUP // 2
    nt = (((1,), (1,)), ((), ()))
    blocks = {}

    def kv(kh, b):
        if (kh, b) not in blocks:
            if b < 0:
                blocks[kh, b] = _kv_blocks(kp_ref[kh], vp_ref[kh], c)
            else:
                rows = slice(b * nq, (b + 1) * nq)
                blocks[kh, b] = _kv_blocks(kc_ref[kh, rows, :], vc_ref[kh, rows, :], c)
        return blocks[kh, b]

    for b in range(row0 // nq, (row0 + nrows) // nq):
        rows = slice(b * nq, (b + 1) * nq)
        if b == 0:
            has_prev = (pl.program_id(0) % (seq // tb)) != 0
            valid = c["jj"] <= c["ii"] + jnp.where(has_prev, nq, 0)
        for kh in range(N_KV_HEADS):
            k_cur, v_cur = kv(kh, b)
            k_prev, v_prev = kv(kh, b - 1)
            for pair in range(pairs):
                hp = kh * pairs + pair
                qp = q_ref[hp, rows, :]
                s_cur = lax.dot_general(qp, k_cur, nt, preferred_element_type=F32)
                s_prev = lax.dot_general(qp, k_prev, nt, preferred_element_type=F32)
                s = jnp.where(c["cur"], s_cur, s_prev)
                if b == 0:
                    s = jnp.where(valid, s, NEG_INF)
                es, sinks = [], []
                for hh in range(2):
                    sh = s[:, hh * nq:(hh + 1) * nq]
                    sink = sink_ref[2 * hp + hh]
                    mx = jnp.maximum(jnp.max(sh, axis=-1, keepdims=True), sink)
                    es.append(jnp.exp(sh - mx))
                    sinks.append(jnp.exp(sink - mx))
                e = jnp.concatenate(es, axis=1).astype(BF16)
                o = _dot(e * c["cur_bf"], v_cur) + _dot(e * c["prev_bf"], v_prev)
                den = o[:, LANES:] + jnp.where(c["first"], sinks[0], sinks[1])
                o_scr[rows, hp * LANES:(hp + 1) * LANES] = (o[:, :LANES] / den).astype(o_scr.dtype)


def _layer_kernel(*refs, mixer, epilogue, tb, seq):
    it = iter(refs)
    h_ref = next(it)
    p_ref = next(it)
    if mixer == "s5":
        y_ref, gmix_ref, d_ref, wglu_ref = next(it), next(it), next(it), next(it)
    else:
        q_ref, kc_ref, kp_ref, vc_ref, vp_ref, sink_ref, wo_ref = (next(it) for _ in range(7))
    wup_ref, wdown_ref, wgate_ref, wproj_ref = (next(it) for _ in range(4))
    if epilogue in ("norm", "final"):
        gnext_ref = next(it)
    else:
        wq_ref, inv_ref = next(it), next(it)
        if epilogue == "kvq":
            wk_ref, wv_ref = next(it), next(it)
    ho_ref = None if epilogue == "final" else next(it)
    if epilogue == "norm":
        hn_ref = next(it)
    elif epilogue == "q":
        qo_ref = next(it)
    elif epilogue == "kvq":
        qo_ref, ko_ref, vo_ref = next(it), next(it), next(it)
    elif epilogue == "final":
        out_ref = next(it)
    if mixer == "attn":
        o_scr = next(it)

    d_model = h_ref.shape[-1]
    d_ff = wup_ref.shape[1]
    sub = min(SUB_BLOCK, tb)
    consts = _attention_consts() if mixer == "attn" else None

    def chain(r0):
        rows = slice(r0, r0 + sub)
        h = h_ref[rows, :]

        if mixer == "s5":
            u = _rms(h, gmix_ref[...])
            ge = _gelu_tanh(y_ref[rows, :] + d_ref[...] * u).astype(BF16)
            mix = []
            for c0 in range(0, d_model, GLU_TILE):
                val = _dot(ge, wglu_ref[:, c0:c0 + GLU_TILE])
                gate = _dot(ge, wglu_ref[:, d_model + c0:d_model + c0 + GLU_TILE])
                mix.append(val * jax.nn.sigmoid(gate))
            h = h + jnp.concatenate(mix, axis=1)
        else:
            _attention(q_ref, kc_ref, kp_ref, vc_ref, vp_ref, sink_ref, o_scr, consts,
                       row0=r0, nrows=sub, tb=tb, seq=seq)
            h = h + _dot(o_scr[rows, :], wo_ref[...])

        s = _inv_rms(h)
        hb = h.astype(BF16)
        acc = None
        for c0 in range(0, d_ff, FF_CHUNK):
            up = _dot(hb, wup_ref[:, c0:c0 + FF_CHUNK])
            act = jnp.square(jnp.maximum(up, 0.0)).astype(BF16)
            part = _dot(act, wdown_ref[c0:c0 + FF_CHUNK, :])
            acc = part if acc is None else acc + part
        h = h + (s * s) * acc

        gate = _inv_rms(h) * _dot(h.astype(BF16), wgate_ref[...])
        proj = _dot(p_ref[rows, :].astype(BF16), wproj_ref[...])
        h = h + jax.nn.sigmoid(gate) * proj

        if epilogue == "final":
            out_ref[rows, :] = _rms(h, gnext_ref[...])
            return
        ho_ref[rows, :] = h
        if epilogue == "norm":
            hn_ref[rows, :] = _rms(h, gnext_ref[...]).astype(hn_ref.dtype)
            return
        cos_t, sin_t, l64 = _rope_tables(inv_ref, pl.program_id(0) * tb + r0, sub, seq)
        hb = h.astype(BF16)
        s = _inv_rms(h)
        q = _dot(hb, wq_ref[...])
        if epilogue == "kvq":
            k = _dot(hb, wk_ref[...])
            v = _dot(hb, wv_ref[...])
        qs = s * HEAD_DIM ** -0.5
        for hp in range(d_model // LANES):
            tile = _rope(q[:, hp * LANES:(hp + 1) * LANES] * qs, cos_t, sin_t, l64)
            qo_ref[hp, rows, :] = tile.astype(qo_ref.dtype)
        if epilogue == "kvq":
            for kh in range(N_KV_HEADS):
                tile = _rope(k[:, kh * LANES:(kh + 1) * LANES] * s, cos_t, sin_t, l64)
                ko_ref[kh, rows, :] = tile.astype(ko_ref.dtype)
                vo_ref[kh, rows, :] = (v[:, kh * LANES:(kh + 1) * LANES] * s).astype(vo_ref.dtype)

    for r0 in range(0, tb, sub):
        chain(r0)


def _weight(w):
    if not isinstance(w, tuple):
        return w, _resident(w.shape)
    arr, idx = w
    nd = arr.ndim - 1
    return arr, pl.BlockSpec((None,) + arr.shape[1:], lambda i: (idx,) + (0,) * nd, pipeline_mode=pl.Buffered(1))


def _layer_call(h, p, mixer_args, tail_args, epi_args, *, mixer, epilogue, seq):
    t, d = h.shape
    tb = min(TOKEN_BLOCK, seq)
    assert seq % tb == 0 and tb % ATTN_BLOCK == 0
    nsteps = t // tb
    row = lambda w: pl.BlockSpec((tb, w), lambda i: (i, 0))
    vec = lambda: pl.BlockSpec((1, d), lambda i: (0, 0))
    heads = lambda n: pl.BlockSpec((n, tb, LANES), lambda i: (0, i, 0))

    args, specs = [], []

    def add(arr, spec):
        args.append(arr)
        specs.append(spec)

    def add_weight(w):
        add(*_weight(w))

    p_all, layer = p
    add(h, row(d))
    add(p_all, pl.BlockSpec((None, tb, p_all.shape[-1]), lambda i: (layer, i, 0)))
    if mixer == "s5":
        y, gmix, dskip, wglu = mixer_args
        add(y, row(d))
        add(gmix, vec())
        add(dskip, vec())
        add_weight(wglu)
    else:
        q, kd, vd, sinks, wo = mixer_args
        blocks_per_tb = tb // ATTN_BLOCK
        prev = lambda: pl.BlockSpec((N_KV_HEADS, ATTN_BLOCK, LANES),
                                    lambda i: (0, jnp.maximum(i * blocks_per_tb - 1, 0), 0))
        add(q, heads(d // LANES))
        add(kd, heads(N_KV_HEADS))
        add(kd, prev())
        add(vd, heads(N_KV_HEADS))
        add(vd, prev())
        add(sinks, pl.BlockSpec(memory_space=pltpu.SMEM))
        add_weight(wo)
    for w in tail_args:
        add_weight(w)

    out_shapes, out_specs = [], []
    if epilogue != "final":
        out_shapes.append(jax.ShapeDtypeStruct((t, d), F32))
        out_specs.append(row(d))
    if epilogue in ("norm", "final"):
        add(epi_args[0], vec())
        out_shapes.append(jax.ShapeDtypeStruct((t, d), F32))
        out_specs.append(row(d))
    else:
        add_weight(epi_args[0])
        add(epi_args[1], pl.BlockSpec((1, LANES), lambda i: (0, 0)))
        out_shapes.append(jax.ShapeDtypeStruct((d // LANES, t, LANES), BF16))
        out_specs.append(heads(d // LANES))
        if epilogue == "kvq":
            add_weight(epi_args[2])
            add_weight(epi_args[3])
            out_shapes += [jax.ShapeDtypeStruct((N_KV_HEADS, t, LANES), BF16)] * 2
            out_specs += [heads(N_KV_HEADS), heads(N_KV_HEADS)]

    scratch = [pltpu.VMEM((tb, d), BF16)] if mixer == "attn" else []
    kern = functools.partial(_layer_kernel, mixer=mixer, epilogue=epilogue, tb=tb, seq=seq)
    return pl.pallas_call(
        kern,
        grid=(nsteps,),
        in_specs=specs,
        out_specs=out_specs,
        out_shape=out_shapes,
        scratch_shapes=scratch,
        compiler_params=pltpu.CompilerParams(
            dimension_semantics=("arbitrary",), vmem_limit_bytes=VMEM_LIMIT_BYTES),
        name=f"layer_{mixer}_{epilogue}",
    )(*args)


def kernel(x, p, norm_mix, ssm_lambda_re, ssm_lambda_im, ssm_log_dt, ssm_b_re, ssm_b_im, ssm_c_re, ssm_c_im, ssm_d, ssm_w_glu, kv_norm, w_k, w_v, w_q, attn_sinks, w_o, norm_mlp, w_up, w_down, norm_ple, w_ple_gate, w_ple_proj, norm_final):
    bsz, seq, d = x.shape
    depth = p.shape[0]
    n_a = ssm_lambda_re.shape[0]
    t = bsz * seq
    L = S5_CHUNK
    assert seq % (L * SUBLANES) == 0 and seq % ATTN_BLOCK == 0 and d % LANES == 0

    vec = lambda g: g.reshape(1, d)
    inv = ROPE_THETA ** (-jnp.arange(0, ROT_DIM, 2, dtype=F32) / ROT_DIM)
    inv64 = jnp.concatenate([inv, inv, jnp.zeros((HEAD_DIM - ROT_DIM,), F32)])
    inv_lanes = jnp.tile(inv64, LANES // HEAD_DIM).reshape(1, LANES)
    dup = lambda w: jnp.repeat(w.reshape(d, N_KV_HEADS, 1, HEAD_DIM), 2, axis=2).reshape(d, 2 * N_KV_HEADS * HEAD_DIM)

    p_all = p.reshape(depth, t, p.shape[-1])
    glu_bf = ssm_w_glu.astype(BF16)
    up_bf = (norm_mlp[:, :, None] * w_up).astype(BF16)
    down_bf = w_down.astype(BF16)
    gate_bf = (norm_ple[:, :, None] * w_ple_gate).astype(BF16)
    proj_bf = w_ple_proj.astype(BF16)
    q_bf = (norm_mix[n_a:, :, None] * w_q).astype(BF16)
    o_bf = w_o.astype(BF16)
    k_bf = dup(kv_norm[:, None] * w_k).astype(BF16)
    v_bf = dup(kv_norm[:, None] * w_v).astype(BF16)

    merge = lambda a: a.reshape((-1,) + a.shape[2:])
    tables = _s5_tables(merge(ssm_lambda_re), merge(ssm_lambda_im), merge(ssm_log_dt), merge(ssm_b_re),
                        merge(ssm_b_im), merge(ssm_c_re), merge(ssm_c_im))

    h = x.reshape(t, d)
    hn = _prenorm(h, vec(norm_mix[0]))
    q = kd = vd = None
    out = None
    for i in range(depth):
        tail = ((up_bf, i), (down_bf, i), (gate_bf, i), (proj_bf, i))
        if i == depth - 1:
            epilogue, epi = "final", (vec(norm_final),)
        elif i + 1 < n_a:
            epilogue, epi = "norm", (vec(norm_mix[i + 1]),)
        elif i + 1 == n_a:
            epilogue, epi = "kvq", ((q_bf, 0), inv_lanes, k_bf, v_bf)
        else:
            epilogue, epi = "q", ((q_bf, i + 1 - n_a), inv_lanes)

        if i < n_a:
            y = _s5_core(hn, *tables, layer=i, chunks_per_seq=seq // L)
            mixer, mix = "s5", (y, vec(norm_mix[i]), vec(ssm_d[i]), (glu_bf, i))
        else:
            j = i - n_a
            mixer, mix = "attn", (q, kd, vd, attn_sinks[j], (o_bf, j))

        res = _layer_call(h, (p_all, i), mix, tail, epi, mixer=mixer, epilogue=epilogue, seq=seq)
        if epilogue == "final":
            out = res[0]
        elif epilogue == "norm":
            h, hn = res
        elif epilogue == "q":
            h, q = res
        else:
            h, q, kd, vd = res
    return out.reshape(bsz, seq, d)
```

```python
import functools
import math

import jax
import jax.numpy as jnp
from jax import lax
from jax.experimental import pallas as pl
from jax.experimental.pallas import tpu as pltpu

SSM_GROUP = 16
SSM_STATE = 64
HEAD_DIM = 64
N_KV_HEADS = 4
GQA_GROUP = 4
ATTN_BLOCK = 128
WINDOW = 128
ROPE_THETA = 500000.0
ROT_DIM = 16
RMS_EPS = 1e-6
NEG_INF = -1e30

LANES = 128
SUBLANES = 8
VMEM_LIMIT_BYTES = 56 * 1024 * 1024

S5_CHUNK = 16
S5_ROWS = 512
SC_ROWS = 5 * SUBLANES
TOKEN_BLOCK = 512
SUB_BLOCK = 256
FF_CHUNK = 1024
GLU_TILE = 256

BF16 = jnp.bfloat16
F32 = jnp.float32


def _dot(a, b):
    return jnp.dot(a, b, preferred_element_type=F32)


def _inv_rms(x):
    return lax.rsqrt(jnp.mean(x * x, axis=-1, keepdims=True) + RMS_EPS)


def _rms(x, g):
    return x * _inv_rms(x) * g


def _gelu_tanh(x):
    c = math.sqrt(2.0 / math.pi)
    return 0.5 * x * (1.0 + jnp.tanh(c * (x + 0.044715 * (x * x * x))))


def _resident(shape):
    nd = len(shape)
    return pl.BlockSpec(shape, lambda i: (0,) * nd, pipeline_mode=pl.Buffered(1))


def _s5_tables(lam_re, lam_im, log_dt, b_re, b_im, c_re, c_im):
    hp = lax.Precision.HIGHEST
    L = S5_CHUNK
    G, N = lam_re.shape
    H = b_re.shape[-1]
    dt = jnp.exp(log_dt)[:, None]
    lr, ph = lam_re * dt, lam_im * dt
    d = jnp.arange(L + 1, dtype=F32)[:, None, None]
    pw_r = jnp.exp(d * lr) * jnp.cos(d * ph)
    pw_i = jnp.exp(d * lr) * jnp.sin(d * ph)
    a_r, a_i = pw_r[1], pw_i[1]
    den = lam_re * lam_re + lam_im * lam_im
    nr = a_r - 1.0
    coef_r = (nr * lam_re + a_i * lam_im) / den
    coef_i = (a_i * lam_re - nr * lam_im) / den
    bb_r = coef_r[..., None] * b_re - coef_i[..., None] * b_im
    bb_i = coef_r[..., None] * b_im + coef_i[..., None] * b_re
    ab_r = pw_r[:L, :, :, None] * bb_r - pw_i[:L, :, :, None] * bb_i
    ab_i = pw_r[:L, :, :, None] * bb_i + pw_i[:L, :, :, None] * bb_r
    kc = (jnp.einsum('dgnh,gkn->ghdk', ab_r, c_re, precision=hp)
          - jnp.einsum('dgnh,gkn->ghdk', ab_i, c_im, precision=hp)).reshape(G, H, L * H)

    p_r = jnp.transpose(ab_r[::-1], (1, 0, 3, 2)).reshape(G // 2, 2, L * H, N)
    p_i = jnp.transpose(ab_i[::-1], (1, 0, 3, 2)).reshape(G // 2, 2, L * H, N)
    zp = jnp.zeros_like(p_r[:, 0])
    pb = jnp.concatenate([jnp.concatenate([p_r[:, 0], zp, p_i[:, 0], zp], axis=-1),
                          jnp.concatenate([zp, p_r[:, 1], zp, p_i[:, 1]], axis=-1)], axis=1)

    ca_r = c_re[None] * pw_r[1:, :, None, :] - c_im[None] * pw_i[1:, :, None, :]
    ca_i = c_re[None] * pw_i[1:, :, None, :] + c_im[None] * pw_r[1:, :, None, :]
    q_r = jnp.transpose(ca_r, (1, 3, 0, 2)).reshape(G // 2, 2, N, L * H)
    q_i = jnp.transpose(-ca_i, (1, 3, 0, 2)).reshape(G // 2, 2, N, L * H)
    zq = jnp.zeros_like(q_r[:, 0])
    qb = jnp.concatenate([jnp.concatenate([q_r[:, 0], zq], axis=-1), jnp.concatenate([zq, q_r[:, 1]], axis=-1),
                          jnp.concatenate([q_i[:, 0], zq], axis=-1), jnp.concatenate([zq, q_i[:, 1]], axis=-1)],
                         axis=1)

    k8 = list(range(SUBLANES))
    powers = [k + 1 for k in k8] + [SUBLANES] * SUBLANES
    keep = [1.0] * (2 * SUBLANES)
    for s in (1, 2, 4):
        powers += [s] * SUBLANES
        keep += [float(k >= s) for k in k8]
    mm = jnp.array(powers, F32)[:, None, None] * L
    keep = jnp.array(keep, F32)[:, None, None]
    sc = jnp.stack([keep * jnp.exp(mm * lr) * jnp.cos(mm * ph), keep * jnp.exp(mm * lr) * jnp.sin(mm * ph)])
    sc = jnp.transpose(sc.reshape(2, SC_ROWS, G // 2, 2 * N), (2, 0, 1, 3))
    return kc, pb.astype(BF16), qb.astype(BF16), sc


def _block_transpose(vs, lane_blk):
    vs = list(vs)
    for d in (4, 2, 1):
        take = (lane_blk & d) != 0
        for r in range(len(vs)):
            if r & d:
                continue
            top, bot = vs[r], vs[r + d]
            if 2 * d * SSM_GROUP == LANES:
                swapped = pltpu.roll(jnp.where(take, top, bot), d * SSM_GROUP, axis=1)
                vs[r] = jnp.where(take, swapped, top)
                vs[r + d] = jnp.where(take, bot, swapped)
            else:
                vs[r] = jnp.where(take, pltpu.roll(bot, d * SSM_GROUP, axis=1), top)
                vs[r + d] = jnp.where(take, bot, pltpu.roll(top, LANES - d * SSM_GROUP, axis=1))
    return vs


def _toeplitz_rows(kc, j, lane):
    lo, hi = kc[:, :LANES], kc[:, LANES:]
    r = (j % SUBLANES) * SSM_GROUP
    if r:
        lo_s, hi_s = pltpu.roll(lo, r, axis=1), pltpu.roll(hi, r, axis=1)
        first = jnp.where(lane < r, 0.0, lo_s)
        second = jnp.where(lane < r, lo_s, hi_s)
    else:
        first, second = lo, hi
    if j < SUBLANES:
        return jnp.concatenate([first, second], axis=1)
    return jnp.concatenate([jnp.zeros_like(first), first], axis=1)


def _s5_core_kernel(*refs, rows, tiles_per_seq):
    L = S5_CHUNK
    hn_ref, kc_ref, p_ref, q_ref, sc_ref, y_ref, t_scr, u_scr, v_scr, x_scr, yg_scr, carry_scr = refs
    n_pair = p_ref.shape[0]
    kk = t_scr.shape[1]
    rb = pl.program_id(1)
    lane_blk = lax.broadcasted_iota(jnp.int32, (rows, LANES), 1) // SSM_GROUP

    @pl.when(rb == 0)
    def _():
        carry_scr[...] = jnp.zeros_like(carry_scr)
        lane = lax.broadcasted_iota(jnp.int32, (SSM_GROUP, LANES), 1)
        for g in range(2 * n_pair):
            kc = kc_ref[g]
            for j in range(L):
                t_scr[g, j * SSM_GROUP:(j + 1) * SSM_GROUP, :] = _toeplitz_rows(kc, j, lane).astype(BF16)

    for jh in range(L // SUBLANES):
        vs = [hn_ref[pl.ds(jh * SUBLANES + r, rows, stride=L), :].astype(BF16) for r in range(SUBLANES)]
        vs = _block_transpose(vs, lane_blk)
        for gl in range(2 * n_pair):
            col = ((gl % 2) * (L // SUBLANES) + jh) * LANES
            u_scr[gl // 2, :, col:col + LANES] = vs[gl]

    for gp in range(n_pair):
        v_scr[gp] = _dot(u_scr[gp], p_ref[gp])

    half = LANES
    row = lax.broadcasted_iota(jnp.int32, (SUBLANES, half), 0)
    tiles = rows // SUBLANES

    def tile_body(t, carry):
        keep = ((rb * tiles + t) % tiles_per_seq) != 0
        r0 = pl.multiple_of(t * SUBLANES, SUBLANES)
        out = []
        for gp in range(n_pair):
            cr = jnp.where(keep, carry[2 * gp], 0.0)
            ci = jnp.where(keep, carry[2 * gp + 1], 0.0)
            r = v_scr[gp, pl.ds(r0, SUBLANES), 0:half]
            i = v_scr[gp, pl.ds(r0, SUBLANES), half:2 * half]
            for s_idx, s in enumerate((1, 2, 4)):
                lo = (2 + s_idx) * SUBLANES
                ar = sc_ref[gp, 0, lo:lo + SUBLANES, :]
                ai = sc_ref[gp, 1, lo:lo + SUBLANES, :]
                rs = pltpu.roll(r, s, axis=0)
                is_ = pltpu.roll(i, s, axis=0)
                r, i = r + (ar * rs - ai * is_), i + (ar * is_ + ai * rs)
            pw_r = sc_ref[gp, 0, 0:SUBLANES, :]
            pw_i = sc_ref[gp, 1, 0:SUBLANES, :]
            inc_r = r + (pw_r * cr - pw_i * ci)
            inc_i = i + (pw_r * ci + pw_i * cr)
            x_scr[gp, pl.ds(r0, SUBLANES), 0:half] = jnp.where(row == 0, cr, pltpu.roll(inc_r, 1, axis=0))
            x_scr[gp, pl.ds(r0, SUBLANES), half:2 * half] = jnp.where(row == 0, ci, pltpu.roll(inc_i, 1, axis=0))
            last_r = jnp.broadcast_to(r[SUBLANES - 1:SUBLANES, :], (SUBLANES, half))
            last_i = jnp.broadcast_to(i[SUBLANES - 1:SUBLANES, :], (SUBLANES, half))
            a8r = sc_ref[gp, 0, SUBLANES:2 * SUBLANES, :]
            a8i = sc_ref[gp, 1, SUBLANES:2 * SUBLANES, :]
            out += [last_r + (a8r * cr - a8i * ci), last_i + (a8r * ci + a8i * cr)]
        return tuple(out)

    init = tuple(carry_scr[k] for k in range(2 * n_pair))
    fin = lax.fori_loop(0, tiles, tile_body, init, unroll=2)
    for k in range(2 * n_pair):
        carry_scr[k] = fin[k]

    for gp in range(n_pair):
        carry_y = _dot(x_scr[gp].astype(BF16), q_ref[gp])
        for g2 in range(2):
            cols = slice(g2 * kk, (g2 + 1) * kk)
            yg_scr[2 * gp + g2] = _dot(u_scr[gp, :, cols], t_scr[2 * gp + g2]) + carry_y[:, cols]

    for ih in range(L // SUBLANES):
        vs = [yg_scr[gl, :, ih * LANES:(ih + 1) * LANES] for gl in range(2 * n_pair)]
        vs = _block_transpose(vs, lane_blk)
        for r in range(SUBLANES):
            y_ref[pl.ds(ih * SUBLANES + r, rows, stride=L), :] = vs[r]


def _s5_core(hn, kc, pb, qb, sc, *, layer, chunks_per_seq):
    tokens, d = hn.shape
    L = S5_CHUNK
    m = tokens // L
    rows = min(S5_ROWS, m)
    kk = L * SSM_GROUP
    n_pair = LANES // (2 * SSM_GROUP)
    n_tiles = d // LANES
    kern = functools.partial(_s5_core_kernel, rows=rows, tiles_per_seq=chunks_per_seq // SUBLANES)
    per_tile = lambda *shape: pl.BlockSpec(shape, lambda lt, rb: (layer * n_tiles + lt,) + (0,) * (len(shape) - 1))
    return pl.pallas_call(
        kern,
        grid=(n_tiles, m // rows),
        in_specs=[
            pl.BlockSpec((rows * L, LANES), lambda lt, rb: (rb, lt)),
            per_tile(2 * n_pair, SSM_GROUP, kk),
            per_tile(n_pair, 2 * kk, kk),
            per_tile(n_pair, kk, 2 * kk),
            per_tile(n_pair, 2, SC_ROWS, LANES),
        ],
        out_specs=pl.BlockSpec((rows * L, LANES), lambda lt, rb: (rb, lt)),
        out_shape=jax.ShapeDtypeStruct((tokens, d), F32),
        scratch_shapes=[
            pltpu.VMEM((2 * n_pair, kk, kk), BF16),
            pltpu.VMEM((n_pair, rows, 2 * kk), BF16),
            pltpu.VMEM((n_pair, rows, 2 * LANES), F32),
            pltpu.VMEM((n_pair, rows, 2 * LANES), F32),
            pltpu.VMEM((2 * n_pair, rows, kk), F32),
            pltpu.VMEM((2 * n_pair, SUBLANES, LANES), F32),
        ],
        compiler_params=pltpu.CompilerParams(
            dimension_semantics=("arbitrary", "arbitrary"), vmem_limit_bytes=VMEM_LIMIT_BYTES),
        name="s5_core",
    )(hn, kc, pb, qb, sc)


def _prenorm_kernel(x_ref, g_ref, o_ref):
    o_ref[...] = _rms(x_ref[...], g_ref[...]).astype(o_ref.dtype)


def _prenorm(x, g):
    t, d = x.shape
    tb = min(t, 1024)
    return pl.pallas_call(
        _prenorm_kernel,
        grid=(t // tb,),
        in_specs=[pl.BlockSpec((tb, d), lambda i: (i, 0)), pl.BlockSpec((1, d), lambda i: (0, 0))],
        out_specs=pl.BlockSpec((tb, d), lambda i: (i, 0)),
        out_shape=jax.ShapeDtypeStruct((t, d), F32),
        compiler_params=pltpu.CompilerParams(dimension_semantics=("arbitrary",)),
        name="prenorm",
    )(x, g)


def _rope(x, cos_t, sin_t, l64):
    half = ROT_DIM // 2
    lo = pltpu.roll(x, half, axis=1)
    hi = pltpu.roll(x, LANES - half, axis=1)
    return x * cos_t + jnp.where(l64 < half, -hi, lo) * sin_t


def _rope_tables(inv_ref, row0, nrows, seq):
    pos = (row0 + lax.broadcasted_iota(jnp.int32, (nrows, LANES), 0)) % seq
    ang = pos.astype(F32) * inv_ref[...]
    l64 = lax.broadcasted_iota(jnp.int32, (nrows, LANES), 1) % HEAD_DIM
    return jnp.cos(ang), jnp.sin(ang), l64


def _attention_consts():
    nq = ATTN_BLOCK
    ii = lax.broadcasted_iota(jnp.int32, (nq, 2 * nq), 0)
    jj = lax.broadcasted_iota(jnp.int32, (nq, 2 * nq), 1) % nq
    cur = jj <= ii
    lane = lax.broadcasted_iota(jnp.int32, (nq, LANES), 1)
    first = lane < HEAD_DIM
    return dict(ii=ii, jj=jj, cur=cur, first=first, zero=jnp.zeros((nq, LANES), BF16),
                cur_bf=jnp.where(cur, 1.0, 0.0).astype(BF16), prev_bf=jnp.where(cur, 0.0, 1.0).astype(BF16),
                ones_a=jnp.where(first, 1.0, 0.0).astype(BF16), ones_b=jnp.where(first, 0.0, 1.0).astype(BF16))


def _kv_blocks(k, v, c):
    first, zero = c["first"], c["zero"]
    k_blk = jnp.concatenate([jnp.where(first, k, zero), jnp.where(first, zero, k)], axis=0)
    v_blk = jnp.concatenate([
        jnp.concatenate([jnp.where(first, v, zero), c["ones_a"]], axis=1),
        jnp.concatenate([jnp.where(first, zero, v), c["ones_b"]], axis=1)], axis=0)
    return k_blk, v_blk


def _attention(q_ref, kc_ref, kp_ref, vc_ref, vp_ref, sink_ref, o_scr, c, *, row0, nrows, tb, seq):
    nq = ATTN_BLOCK
    pairs = GQA_GROUP // 2
    nt = (((1,), (1,)), ((), ()))
    blocks = {}

    def kv(kh, b):
        if (kh, b) not in blocks:
            if b < 0:
                blocks[kh, b] = _kv_blocks(kp_ref[kh], vp_ref[kh], c)
            else:
                rows = slice(b * nq, (b + 1) * nq)
                blocks[kh, b] = _kv_blocks(kc_ref[kh, rows, :], vc_ref[kh, rows, :], c)
        return blocks[kh, b]

    for b in range(row0 // nq, (row0 + nrows) // nq):
        rows = slice(b * nq, (b + 1) * nq)
        if b == 0:
            has_prev = (pl.program_id(0) % (seq // tb)) != 0
            valid = c["jj"] <= c["ii"] + jnp.where(has_prev, nq, 0)
        for kh in range(N_KV_HEADS):
            k_cur, v_cur = kv(kh, b)
            k_prev, v_prev = kv(kh, b - 1)
            for pair in range(pairs):
                hp = kh * pairs + pair
                qp = q_ref[hp, rows, :]
                s_cur = lax.dot_general(qp, k_cur, nt, preferred_element_type=F32)
                s_prev = lax.dot_general(qp, k_prev, nt, preferred_element_type=F32)
                s = jnp.where(c["cur"], s_cur, s_prev)
                if b == 0:
                    s = jnp.where(valid, s, NEG_INF)
                es, sinks = [], []
                for hh in range(2):
                    sh = s[:, hh * nq:(hh + 1) * nq]
                    sink = sink_ref[2 * hp + hh]
                    mx = jnp.maximum(jnp.max(sh, axis=-1, keepdims=True), sink)
                    es.append(jnp.exp(sh - mx))
                    sinks.append(jnp.exp(sink - mx))
                e = jnp.concatenate(es, axis=1).astype(BF16)
                o = _dot(e * c["cur_bf"], v_cur) + _dot(e * c["prev_bf"], v_prev)
                den = o[:, LANES:] + jnp.where(c["first"], sinks[0], sinks[1])
                o_scr[rows, hp * LANES:(hp + 1) * LANES] = (o[:, :LANES] / den).astype(o_scr.dtype)


def _layer_kernel(*refs, mixer, epilogue, tb, seq):
    it = iter(refs)
    h_ref = next(it)
    p_ref = next(it)
    if mixer == "s5":
        y_ref, gmix_ref, d_ref, wglu_ref = next(it), next(it), next(it), next(it)
    else:
        q_ref, kc_ref, kp_ref, vc_ref, vp_ref, sink_ref, wo_ref = (next(it) for _ in range(7))
    wup_ref, wdown_ref, wgate_ref, wproj_ref = (next(it) for _ in range(4))
    if epilogue in ("norm", "final"):
        gnext_ref = next(it)
    else:
        wq_ref, inv_ref = next(it), next(it)
        if epilogue == "kvq":
            wk_ref, wv_ref = next(it), next(it)
    ho_ref = None if epilogue == "final" else next(it)
    if epilogue == "norm":
        hn_ref = next(it)
    elif epilogue == "q":
        qo_ref = next(it)
    elif epilogue == "kvq":
        qo_ref, ko_ref, vo_ref = next(it), next(it), next(it)
    elif epilogue == "final":
        out_ref = next(it)
    if mixer == "attn":
        o_scr = next(it)

    d_model = h_ref.shape[-1]
    d_ff = wup_ref.shape[1]
    sub = min(SUB_BLOCK, tb)
    consts = _attention_consts() if mixer == "attn" else None

    def chain(r0):
        rows = slice(r0, r0 + sub)
        h = h_ref[rows, :]

        if mixer == "s5":
            u = _rms(h, gmix_ref[...])
            ge = _gelu_tanh(y_ref[rows, :] + d_ref[...] * u).astype(BF16)
            mix = []
            for c0 in range(0, d_model, GLU_TILE):
                val = _dot(ge, wglu_ref[:, c0:c0 + GLU_TILE])
                gate = _dot(ge, wglu_ref[:, d_model + c0:d_model + c0 + GLU_TILE])
                mix.append(val * jax.nn.sigmoid(gate))
            h = h + jnp.concatenate(mix, axis=1)
        else:
            _attention(q_ref, kc_ref, kp_ref, vc_ref, vp_ref, sink_ref, o_scr, consts,
                       row0=r0, nrows=sub, tb=tb, seq=seq)
            h = h + _dot(o_scr[rows, :], wo_ref[...])

        s = _inv_rms(h)
        hb = h.astype(BF16)
        acc = None
        for c0 in range(0, d_ff, FF_CHUNK):
            up = _dot(hb, wup_ref[:, c0:c0 + FF_CHUNK])
            act = jnp.square(jnp.maximum(up, 0.0)).astype(BF16)
            part = _dot(act, wdown_ref[c0:c0 + FF_CHUNK, :])
            acc = part if acc is None else acc + part
        h = h + (s * s) * acc

        gate = _inv_rms(h) * _dot(h.astype(BF16), wgate_ref[...])
        proj = _dot(p_ref[rows, :].astype(BF16), wproj_ref[...])
        h = h + jax.nn.sigmoid(gate) * proj

        if epilogue == "final":
            out_ref[rows, :] = _rms(h, gnext_ref[...])
            return
        ho_ref[rows, :] = h
        if epilogue == "norm":
            hn_ref[rows, :] = _rms(h, gnext_ref[...]).astype(hn_ref.dtype)
            return
        cos_t, sin_t, l64 = _rope_tables(inv_ref, pl.program_id(0) * tb + r0, sub, seq)
        hb = h.astype(BF16)
        s = _inv_rms(h)
        q = _dot(hb, wq_ref[...])
        if epilogue == "kvq":
            k = _dot(hb, wk_ref[...])
            v = _dot(hb, wv_ref[...])
        qs = s * HEAD_DIM ** -0.5
        for hp in range(d_model // LANES):
            tile = _rope(q[:, hp * LANES:(hp + 1) * LANES] * qs, cos_t, sin_t, l64)
            qo_ref[hp, rows, :] = tile.astype(qo_ref.dtype)
        if epilogue == "kvq":
            first = lax.broadcasted_iota(jnp.int32, (sub, LANES), 1) < HEAD_DIM
            for t in range(N_KV_HEADS * HEAD_DIM // LANES):
                kt = _rope(k[:, t * LANES:(t + 1) * LANES] * s, cos_t, sin_t, l64)
                vt = v[:, t * LANES:(t + 1) * LANES] * s
                k_sw = pltpu.roll(kt, HEAD_DIM, axis=1)
                v_sw = pltpu.roll(vt, HEAD_DIM, axis=1)
                ko_ref[2 * t, rows, :] = jnp.where(first, kt, k_sw).astype(ko_ref.dtype)
                ko_ref[2 * t + 1, rows, :] = jnp.where(first, k_sw, kt).astype(ko_ref.dtype)
                vo_ref[2 * t, rows, :] = jnp.where(first, vt, v_sw).astype(vo_ref.dtype)
                vo_ref[2 * t + 1, rows, :] = jnp.where(first, v_sw, vt).astype(vo_ref.dtype)

    for r0 in range(0, tb, sub):
        chain(r0)


def _weight(w):
    if not isinstance(w, tuple):
        return w, _resident(w.shape)
    arr, idx = w
    nd = arr.ndim - 1
    return arr, pl.BlockSpec((None,) + arr.shape[1:], lambda i: (idx,) + (0,) * nd, pipeline_mode=pl.Buffered(1))


def _layer_call(h, p, mixer_args, tail_args, epi_args, *, mixer, epilogue, seq):
    t, d = h.shape
    tb = min(TOKEN_BLOCK, seq)
    assert seq % tb == 0 and tb % ATTN_BLOCK == 0
    nsteps = t // tb
    row = lambda w: pl.BlockSpec((tb, w), lambda i: (i, 0))
    vec = lambda: pl.BlockSpec((1, d), lambda i: (0, 0))
    heads = lambda n: pl.BlockSpec((n, tb, LANES), lambda i: (0, i, 0))

    args, specs = [], []

    def add(arr, spec):
        args.append(arr)
        specs.append(spec)

    def add_weight(w):
        add(*_weight(w))

    p_all, layer = p
    add(h, row(d))
    add(p_all, pl.BlockSpec((None, tb, p_all.shape[-1]), lambda i: (layer, i, 0)))
    if mixer == "s5":
        y, gmix, dskip, wglu = mixer_args
        add(y, row(d))
        add(gmix, vec())
        add(dskip, vec())
        add_weight(wglu)
    else:
        q, kd, vd, sinks, wo = mixer_args
        blocks_per_tb = tb // ATTN_BLOCK
        prev = lambda: pl.BlockSpec((N_KV_HEADS, ATTN_BLOCK, LANES),
                                    lambda i: (0, jnp.maximum(i * blocks_per_tb - 1, 0), 0))
        add(q, heads(d // LANES))
        add(kd, heads(N_KV_HEADS))
        add(kd, prev())
        add(vd, heads(N_KV_HEADS))
        add(vd, prev())
        add(sinks, pl.BlockSpec(memory_space=pltpu.SMEM))
        add_weight(wo)
    for w in tail_args:
        add_weight(w)

    out_shapes, out_specs = [], []
    if epilogue != "final":
        out_shapes.append(jax.ShapeDtypeStruct((t, d), F32))
        out_specs.append(row(d))
    if epilogue in ("norm", "final"):
        add(epi_args[0], vec())
        out_shapes.append(jax.ShapeDtypeStruct((t, d), F32))
        out_specs.append(row(d))
    else:
        add_weight(epi_args[0])
        add(epi_args[1], pl.BlockSpec((1, LANES), lambda i: (0, 0)))
        out_shapes.append(jax.ShapeDtypeStruct((d // LANES, t, LANES), BF16))
        out_specs.append(heads(d // LANES))
        if epilogue == "kvq":
            add_weight(epi_args[2])
            add_weight(epi_args[3])
            out_shapes += [jax.ShapeDtypeStruct((N_KV_HEADS, t, LANES), BF16)] * 2
            out_specs += [heads(N_KV_HEADS), heads(N_KV_HEADS)]

    scratch = [pltpu.VMEM((tb, d), BF16)] if mixer == "attn" else []
    kern = functools.partial(_layer_kernel, mixer=mixer, epilogue=epilogue, tb=tb, seq=seq)
    return pl.pallas_call(
        kern,
        grid=(nsteps,),
        in_specs=specs,
        out_specs=out_specs,
        out_shape=out_shapes,
        scratch_shapes=scratch,
        compiler_params=pltpu.CompilerParams(
            dimension_semantics=("arbitrary",), vmem_limit_bytes=VMEM_LIMIT_BYTES),
        name=f"layer_{mixer}_{epilogue}",
    )(*args)


def kernel(x, p, norm_mix, ssm_lambda_re, ssm_lambda_im, ssm_log_dt, ssm_b_re, ssm_b_im, ssm_c_re, ssm_c_im, ssm_d, ssm_w_glu, kv_norm, w_k, w_v, w_q, attn_sinks, w_o, norm_mlp, w_up, w_down, norm_ple, w_ple_gate, w_ple_proj, norm_final):
    bsz, seq, d = x.shape
    depth = p.shape[0]
    n_a = ssm_lambda_re.shape[0]
    t = bsz * seq
    L = S5_CHUNK
    assert seq % (L * SUBLANES) == 0 and seq % ATTN_BLOCK == 0 and d % LANES == 0

    vec = lambda g: g.reshape(1, d)
    inv = ROPE_THETA ** (-jnp.arange(0, ROT_DIM, 2, dtype=F32) / ROT_DIM)
    inv64 = jnp.concatenate([inv, inv, jnp.zeros((HEAD_DIM - ROT_DIM,), F32)])
    inv_lanes = jnp.tile(inv64, LANES // HEAD_DIM).reshape(1, LANES)

    p_all = p.reshape(depth, t, p.shape[-1])
    glu_bf = ssm_w_glu.astype(BF16)
    up_bf = (norm_mlp[:, :, None] * w_up).astype(BF16)
    down_bf = w_down.astype(BF16)
    gate_bf = (norm_ple[:, :, None] * w_ple_gate).astype(BF16)
    proj_bf = w_ple_proj.astype(BF16)
    q_bf = (norm_mix[n_a:, :, None] * w_q).astype(BF16)
    o_bf = w_o.astype(BF16)
    k_bf = (kv_norm[:, None] * w_k).astype(BF16)
    v_bf = (kv_norm[:, None] * w_v).astype(BF16)

    merge = lambda a: a.reshape((-1,) + a.shape[2:])
    tables = _s5_tables(merge(ssm_lambda_re), merge(ssm_lambda_im), merge(ssm_log_dt), merge(ssm_b_re),
                        merge(ssm_b_im), merge(ssm_c_re), merge(ssm_c_im))

    h = x.reshape(t, d)
    hn = _prenorm(h, vec(norm_mix[0]))
    q = kd = vd = None
    out = None
    for i in range(depth):
        tail = ((up_bf, i), (down_bf, i), (gate_bf, i), (proj_bf, i))
        if i == depth - 1:
            epilogue, epi = "final", (vec(norm_final),)
        elif i + 1 < n_a:
            epilogue, epi = "norm", (vec(norm_mix[i + 1]),)
        elif i + 1 == n_a:
            epilogue, epi = "kvq", ((q_bf, 0), inv_lanes, k_bf, v_bf)
        else:
            epilogue, epi = "q", ((q_bf, i + 1 - n_a), inv_lanes)

        if i < n_a:
            y = _s5_core(hn, *tables, layer=i, chunks_per_seq=seq // L)
            mixer, mix = "s5", (y, vec(norm_mix[i]), vec(ssm_d[i]), (glu_bf, i))
        else:
            j = i - n_a
            mixer, mix = "attn", (q, kd, vd, attn_sinks[j], (o_bf, j))

        res = _layer_call(h, (p_all, i), mix, tail, epi, mixer=mixer, epilogue=epilogue, seq=seq)
        if epilogue == "final":
            out = res[0]
        elif epilogue == "norm":
            h, hn = res
        elif epilogue == "q":
            h, q = res
        else:
            h, q, kd, vd = res
    return out.reshape(bsz, seq, d)
```
